```python
import jax, jax.numpy as jnp
from jax import lax
import numpy as np

D_MODEL = 1024
BATCH = 4
SEQ = 8192
DEPTH = 4

GRID_W = 64
CTX_LEN = 256
A_WIDTH = 512
B_WIDTH = 512
A_CONV = 3
B_CONV = 31
AB_IN = 3 * A_WIDTH + 2 * B_WIDTH
AB_OUT = A_WIDTH + B_WIDTH
N_HEADS = 16
N_KV_HEADS = 4
HEAD_DIM = 64
Q_GROUP = N_HEADS // N_KV_HEADS
Q_W = N_HEADS * HEAD_DIM
KV_W = N_KV_HEADS * HEAD_DIM
WINDOW = 128
BLOCK = 128
ROPE_THETA = 10000.0
D_FF = 2816
FFN_CONV = 3
EPS = 1e-6
NEG_INF = -1e30
N_EVEN = (DEPTH + 1) // 2
N_ODD = DEPTH // 2

kernel_name = 'hybrid_conv_swa_diffusion_block'


def rmsnorm(x, g):
    xf = x.astype(jnp.float32)
    y = xf * lax.rsqrt(jnp.mean(xf * xf, axis=-1, keepdims=True) + EPS)
    return (y * g.astype(jnp.float32)).astype(x.dtype)


def layernorm(x, g, b):
    xf = x.astype(jnp.float32)
    mu = jnp.mean(xf, axis=-1, keepdims=True)
    var = jnp.mean(jnp.square(xf - mu), axis=-1, keepdims=True)
    y = (xf - mu) * lax.rsqrt(var + EPS)
    return (y * g.astype(jnp.float32) + b.astype(jnp.float32)).astype(x.dtype)


def modulate(x, shift, scale):
    return x * (1.0 + scale) + shift


def adaln(cond, w_mod, b_mod):
    m = jax.nn.silu(cond) @ w_mod + b_mod
    return jnp.split(m, 6, axis=-1)


def dwconv(x, w):
    k = w.shape[0]
    return lax.conv_general_dilated(
        x, w[:, None, :], window_strides=(1,), padding=[(k // 2, k // 2)],
        dimension_numbers=('NWC', 'WIO', 'NWC'), feature_group_count=x.shape[-1])


def axial_rope_tables(length):
    rows = length // GRID_W
    row = jnp.repeat(jnp.arange(rows), GRID_W).astype(jnp.float32)
    col = jnp.tile(jnp.arange(GRID_W), rows).astype(jnp.float32)
    n_freq = HEAD_DIM // 4
    inv_freq = ROPE_THETA ** (-jnp.arange(n_freq, dtype=jnp.float32) / n_freq)
    ang = jnp.concatenate([row[:, None] * inv_freq, col[:, None] * inv_freq], axis=-1)
    return jnp.cos(ang)[:, None, :], jnp.sin(ang)[:, None, :]


def apply_rope(x, cos, sin):
    xf = x.astype(jnp.float32)
    x1, x2 = jnp.split(xf, 2, axis=-1)
    return jnp.concatenate([x1 * cos - x2 * sin, x2 * cos + x1 * sin], axis=-1).astype(x.dtype)


def sink_softmax(logits, sink):
    full = jnp.concatenate([logits, jnp.broadcast_to(sink, logits.shape[:-1] + (1,))], axis=-1)
    return jax.nn.softmax(full, axis=-1)[..., :-1]


def conv_mixers(h, w_in, conv_a, conv_b, conv_b_bias, ln_g, ln_b, w_out):
    p = h @ w_in
    g_b, g_c, u_a, v_b, gate_b = jnp.split(
        p, [A_WIDTH, 2 * A_WIDTH, 3 * A_WIDTH, 3 * A_WIDTH + B_WIDTH], axis=-1)
    y_a = g_b * dwconv(g_c * u_a, conv_a)
    u = v_b * jax.nn.sigmoid(gate_b)
    u = dwconv(u, conv_b) + conv_b_bias
    y_b = jax.nn.silu(layernorm(u, ln_g, ln_b))
    return jnp.concatenate([y_a, y_b], axis=-1) @ w_out


def windowed_gqa(h, hc, w_qkv, w_o, sinks, need_ctx_out):
    bsz, length, _ = h.shape
    n_ctx = hc.shape[1]
    scale = HEAD_DIM ** -0.5
    q, k, v = jnp.split(h @ w_qkv, [Q_W, Q_W + KV_W], axis=-1)
    q = q.reshape(bsz, length, N_HEADS, HEAD_DIM)
    k = k.reshape(bsz, length, N_KV_HEADS, HEAD_DIM)
    v = v.reshape(bsz, length, N_KV_HEADS, HEAD_DIM)
    cos, sin = axial_rope_tables(length)
    q = apply_rope(q, cos, sin) * scale
    k = apply_rope(k, cos, sin)
    kc, vc = jnp.split(hc @ w_qkv[:, Q_W:], [KV_W], axis=-1)
    kc = kc.reshape(bsz, n_ctx, N_KV_HEADS, HEAD_DIM)
    vc = vc.reshape(bsz, n_ctx, N_KV_HEADS, HEAD_DIM)
    sink = sinks.astype(jnp.float32).reshape(1, N_KV_HEADS, Q_GROUP, 1, 1)

    nblk = length // BLOCK
    qb = q.reshape(bsz, nblk, BLOCK, N_KV_HEADS, Q_GROUP, HEAD_DIM)

    def band(t):
        tb = t.reshape(bsz, nblk, BLOCK, N_KV_HEADS, HEAD_DIM)
        tp = jnp.pad(tb, ((0, 0), (1, 1), (0, 0), (0, 0), (0, 0)))
        return jnp.concatenate([tp[:, :-2], tp[:, 1:-1], tp[:, 2:]], axis=2)

    k_band, v_band = band(k), band(v)
    blk = jnp.arange(nblk)[:, None, None]
    q_pos = blk * BLOCK + jnp.arange(BLOCK)[None, :, None]
    k_pos = (blk - 1) * BLOCK + jnp.arange(3 * BLOCK)[None, None, :]
    mask = (jnp.abs(q_pos - k_pos) <= WINDOW) & (k_pos >= 0) & (k_pos < length)

    def attend_block(args):
        q_blk, k_blk, v_blk, m = args
        s_loc = jnp.einsum('bqhgd,bshd->bhgqs', q_blk, k_blk).astype(jnp.float32)
        s_loc = jnp.where(m, s_loc, NEG_INF)
        s_ctx = jnp.einsum('bqhgd,bchd->bhgqc', q_blk, kc).astype(jnp.float32)
        p = sink_softmax(jnp.concatenate([s_loc, s_ctx], axis=-1), sink)
        p_loc = p[..., :3 * BLOCK].astype(v_blk.dtype)
        p_ctx = p[..., 3 * BLOCK:].astype(vc.dtype)
        return (jnp.einsum('bhgqs,bshd->bqhgd', p_loc, v_blk)
                + jnp.einsum('bhgqc,bchd->bqhgd', p_ctx, vc))

    out = lax.map(attend_block, (jnp.moveaxis(qb, 1, 0), jnp.moveaxis(k_band, 1, 0),
                                 jnp.moveaxis(v_band, 1, 0), mask))
    y = jnp.moveaxis(out, 0, 1).reshape(bsz, length, Q_W) @ w_o

    yc = None
    if need_ctx_out:
        qc = (hc @ w_qkv[:, :Q_W]).reshape(bsz, n_ctx, N_KV_HEADS, Q_GROUP, HEAD_DIM) * scale
        s = jnp.einsum('bqhgd,bchd->bhgqc', qc, kc).astype(jnp.float32)
        p = sink_softmax(s, sink).astype(vc.dtype)
        yc = jnp.einsum('bhgqc,bchd->bqhgd', p, vc).reshape(bsz, n_ctx, Q_W) @ w_o
    return y, yc


def conv_ffn(h, w_up, conv, w_down):
    u = dwconv(h @ w_up, conv)
    a, g = jnp.split(u, 2, axis=-1)
    return (jax.nn.silu(g) * a) @ w_down


def setup_inputs(seed: int = 0) -> dict:
    key = jax.random.key(seed)
    ks = jax.random.split(key, 22)

    def nrm(k, shape, s):
        return jax.random.normal(k, shape, jnp.float32) * s

    return {
        'x': nrm(ks[0], (BATCH, SEQ, D_MODEL), 1.0),
        'c': nrm(ks[1], (BATCH, D_MODEL), 1.0),
        'ctx': nrm(ks[2], (BATCH, CTX_LEN, D_MODEL), 1.0),
        'c_ctx': nrm(ks[3], (D_MODEL,), 1.0),
        'w_mod': nrm(ks[4], (DEPTH, D_MODEL, 6 * D_MODEL), 0.5 * D_MODEL ** -0.5),
        'b_mod': nrm(ks[5], (DEPTH, 6 * D_MODEL), 0.02),
        'norm_mix': 1.0 + nrm(ks[6], (DEPTH, D_MODEL), 0.02),
        'norm_ffn': 1.0 + nrm(ks[7], (DEPTH, D_MODEL), 0.02),
        'w_in_ab': nrm(ks[8], (N_EVEN, D_MODEL, AB_IN), D_MODEL ** -0.5),
        'conv_a': nrm(ks[9], (N_EVEN, A_CONV, A_WIDTH), A_CONV ** -0.5),
        'conv_b': nrm(ks[10], (N_EVEN, B_CONV, B_WIDTH), B_CONV ** -0.5),
        'conv_b_bias': nrm(ks[11], (N_EVEN, B_WIDTH), 0.02),
        'ln_b_gain': 1.0 + nrm(ks[12], (N_EVEN, B_WIDTH), 0.02),
        'ln_b_bias': nrm(ks[13], (N_EVEN, B_WIDTH), 0.02),
        'w_out_ab': nrm(ks[14], (N_EVEN, AB_OUT, D_MODEL), AB_OUT ** -0.5),
        'w_qkv': nrm(ks[15], (N_ODD, D_MODEL, Q_W + 2 * KV_W), D_MODEL ** -0.5),
        'w_o': nrm(ks[16], (N_ODD, Q_W, D_MODEL), Q_W ** -0.5),
        'sinks': nrm(ks[17], (N_ODD, N_HEADS), 1.0),
        'w_up': nrm(ks[18], (DEPTH, D_MODEL, 2 * D_FF), D_MODEL ** -0.5),
        'w_conv_ffn': nrm(ks[19], (DEPTH, FFN_CONV, 2 * D_FF), FFN_CONV ** -0.5),
        'w_down': nrm(ks[20], (DEPTH, D_FF, D_MODEL), D_FF ** -0.5),
        'final_norm': 1.0 + nrm(ks[21], (D_MODEL,), 0.02),
    }


def reference(x, c, ctx, c_ctx, w_mod, b_mod, norm_mix, norm_ffn, w_in_ab, conv_a, conv_b,
              conv_b_bias, ln_b_gain, ln_b_bias, w_out_ab, w_qkv, w_o, sinks, w_up, w_conv_ffn,
              w_down, final_norm):
    xc = ctx
    for l in range(DEPTH):
        last = l == DEPTH - 1
        sh1, sc1, g1, sh2, sc2, g2 = adaln(c[:, None, :], w_mod[l], b_mod[l])
        csh1, csc1, cg1, csh2, csc2, cg2 = adaln(c_ctx, w_mod[l], b_mod[l])
        h = modulate(rmsnorm(x, norm_mix[l]), sh1, sc1)
        hc = modulate(rmsnorm(xc, norm_mix[l]), csh1, csc1)
        if l % 2 == 0:
            e = l // 2
            y = conv_mixers(h, w_in_ab[e], conv_a[e], conv_b[e], conv_b_bias[e],
                            ln_b_gain[e], ln_b_bias[e], w_out_ab[e])
            yc = None if last else conv_mixers(hc, w_in_ab[e], conv_a[e], conv_b[e], conv_b_bias[e],
                                               ln_b_gain[e], ln_b_bias[e], w_out_ab[e])
        else:
            o = l // 2
            y, yc = windowed_gqa(h, hc, w_qkv[o], w_o[o], sinks[o], not last)
        x = x + g1 * y
        x = x + g2 * conv_ffn(modulate(rmsnorm(x, norm_ffn[l]), sh2, sc2), w_up[l], w_conv_ffn[l], w_down[l])
        if not last:
            xc = xc + cg1 * yc
            xc = xc + cg2 * conv_ffn(modulate(rmsnorm(xc, norm_ffn[l]), csh2, csc2),
                                     w_up[l], w_conv_ffn[l], w_down[l])
    return rmsnorm(x, final_norm)
```

```python
import functools

import jax
import jax.numpy as jnp
from jax import lax
from jax.experimental import pallas as pl
from jax.experimental.pallas import tpu as pltpu

F32 = jnp.float32
BF16 = jnp.bfloat16

EPS = 1e-6
NEG_INF = -1e30
GRID_W = 64
ROPE_THETA = 10000.0
HEAD_DIM = 64
N_KV_HEADS = 4
Q_GROUP = 4
BLOCK = 128
A_CONV = 3
B_CONV = 31
FFN_CONV = 3

LANES = 128
VMEM_LIMIT = 56 * 1024 * 1024

NT_DIMS = (((1,), (1,)), ((), ()))
TN_DIMS = (((0,), (0,)), ((), ()))


def _const_spec(shape):
    return pl.BlockSpec(shape, lambda *_: (0,) * len(shape), pipeline_mode=pl.Buffered(1))


def _norm_mod(x, gain, shift, scale):
    ms = jnp.mean(x * x, axis=-1, keepdims=True)
    y = x * lax.rsqrt(ms + EPS) * gain
    return y * (1.0 + scale) + shift


def _silu(x):
    return x / (1.0 + jnp.exp(-x))


def _mod_kernel(c_ref, w_ref, b_ref, o_ref):
    s = _silu(c_ref[...]).astype(BF16)
    o_ref[0] = jnp.dot(s, w_ref[0].astype(BF16), preferred_element_type=F32) + b_ref[0]


def _modulation(cond, w_mod, b_mod):
    depth, d, n = w_mod.shape
    nb = n // 4
    rows = cond.shape[0]
    return pl.pallas_call(
        _mod_kernel,
        grid=(depth, n // nb),
        in_specs=[
            pl.BlockSpec((rows, d), lambda l, j: (0, 0)),
            pl.BlockSpec((1, d, nb), lambda l, j: (l, 0, j)),
            pl.BlockSpec((1, 1, nb), lambda l, j: (l, 0, j)),
        ],
        out_specs=pl.BlockSpec((1, rows, nb), lambda l, j: (l, 0, j)),
        out_shape=jax.ShapeDtypeStruct((depth, rows, n), F32),
        compiler_params=pltpu.CompilerParams(
            dimension_semantics=("arbitrary", "arbitrary"), vmem_limit_bytes=VMEM_LIMIT),
        name="adaln_mod",
    )(cond, w_mod, b_mod.reshape(depth, 1, n))


def _fill_h(h_scr, xp_ref, x_ref, xn_ref, gain, shift, scale, T, halo):
    i = pl.program_id(1)
    last = pl.num_programs(1) - 1
    rb = 32
    for r in range(T // rb):
        h_scr[r * rb:(r + 1) * rb, :] = _norm_mod(
            x_ref[0, r * rb:(r + 1) * rb, :], gain, shift, scale).astype(BF16)
    hp = jnp.where(i > 0, _norm_mod(xp_ref[0], gain, shift, scale), 0.0)
    hn = jnp.where(i < last, _norm_mod(xn_ref[0], gain, shift, scale), 0.0)
    if halo == 8:
        h_scr[T:T + 16, :] = jnp.concatenate([hp, hn], axis=0).astype(BF16)
    else:
        h_scr[T:T + halo, :] = hp.astype(BF16)
        h_scr[T + halo:T + 2 * halo, :] = hn.astype(BF16)


def _halo_specs(T, halo, d, n_seq_blocks):
    per = T // halo
    return [
        pl.BlockSpec((1, halo, d), lambda b, i: (b, jnp.maximum(i * per - 1, 0), 0)),
        pl.BlockSpec((1, T, d), lambda b, i: (b, i, 0)),
        pl.BlockSpec((1, halo, d), lambda b, i: (b, jnp.minimum((i + 1) * per, n_seq_blocks * per - 1), 0)),
    ]


def _mod_spec(mod, d):
    if mod.shape[0] == 1:
        return pl.BlockSpec((1, 6, d), lambda b, i: (0, 0, 0))
    return pl.BlockSpec((1, 6, d), lambda b, i: (b, 0, 0))


def _ffn_kernel(xp_ref, x_ref, xn_ref, mod_ref, nw_ref, wup_ref, cw_ref, wdn_ref, *rest,
                T, n_chunks, CW, final):
    if final:
        fw_ref, o_ref, h_scr, u_scr, act_scr = rest
    else:
        o_ref, h_scr, u_scr, act_scr = rest
    shift = mod_ref[0, 3:4, :]
    scale = mod_ref[0, 4:5, :]
    gate = mod_ref[0, 5:6, :]
    _fill_h(h_scr, xp_ref, x_ref, xn_ref, nw_ref[...], shift, scale, T, 8)

    ncb = CW // LANES
    rb = 64

    def up(c):
        slot = c % 2
        u = jnp.dot(h_scr[...], wup_ref[c], preferred_element_type=F32)
        for j in range(2 * ncb):
            uj = u[:, j * LANES:(j + 1) * LANES]
            u_scr[slot, j, 8:8 + T, :] = uj[0:T]
            u_scr[slot, j, 0:8, :] = uj[T:T + 8]
            u_scr[slot, j, 8 + T:16 + T, :] = uj[T + 8:T + 16]

    def convact(c):
        slot = c % 2
        cw = cw_ref[c]
        for j in range(ncb):
            wa = cw[:, j * LANES:(j + 1) * LANES]
            wg = cw[:, CW + j * LANES:CW + (j + 1) * LANES]
            for r in range(T // rb):
                r0 = r * rb

                def conv(jj, w):
                    return (w[0:1] * u_scr[slot, jj, 7 + r0:7 + r0 + rb, :]
                            + w[1:2] * u_scr[slot, jj, 8 + r0:8 + r0 + rb, :]
                            + w[2:3] * u_scr[slot, jj, 9 + r0:9 + r0 + rb, :])

                a = conv(j, wa)
                g = conv(ncb + j, wg)
                act_scr[r0:r0 + rb, c * CW + j * LANES:c * CW + (j + 1) * LANES] = (_silu(g) * a).astype(BF16)

    up(0)
    for c in range(n_chunks):
        if c + 1 < n_chunks:
            up(c + 1)
        convact(c)

    y = jnp.dot(act_scr[...], wdn_ref[...], preferred_element_type=F32)
    for r in range(T // rb):
        out = x_ref[0, r * rb:(r + 1) * rb, :] + gate * y[r * rb:(r + 1) * rb, :]
        if final:
            ms = jnp.mean(out * out, axis=-1, keepdims=True)
            out = out * lax.rsqrt(ms + EPS) * fw_ref[...]
        o_ref[0, r * rb:(r + 1) * rb, :] = out


def _ffn(x, mod, norm_w, wup, cw, wdn, final_w, T):
    bsz, seq, d = x.shape
    n_chunks, _, cw2 = wup.shape
    CW = cw2 // 2
    d_ff = n_chunks * CW
    nT = seq // T
    final = final_w is not None
    in_specs = _halo_specs(T, 8, d, nT) + [
        _mod_spec(mod, d),
        _const_spec((1, d)),
        _const_spec((n_chunks, d, 2 * CW)),
        _const_spec((n_chunks, FFN_CONV, 2 * CW)),
        _const_spec((d_ff, d)),
    ]
    args = [x, x, x, mod, norm_w.reshape(1, d), wup, cw, wdn]
    if final:
        in_specs.append(_const_spec((1, d)))
        args.append(final_w.reshape(1, d))
    return pl.pallas_call(
        functools.partial(_ffn_kernel, T=T, n_chunks=n_chunks, CW=CW, final=final),
        grid=(bsz, nT),
        in_specs=in_specs,
        out_specs=pl.BlockSpec((1, T, d), lambda b, i: (b, i, 0)),
        out_shape=jax.ShapeDtypeStruct((bsz, seq, d), F32),
        scratch_shapes=[
            pltpu.VMEM((T + 16, d), BF16),
            pltpu.VMEM((2, 2 * CW // LANES, T + 16, LANES), F32),
            pltpu.VMEM((T, d_ff), BF16),
        ],
        compiler_params=pltpu.CompilerParams(
            dimension_semantics=("arbitrary", "arbitrary"), vmem_limit_bytes=VMEM_LIMIT),
        name="conv_ffn",
    )(*args)


def _mixer_kernel(xp_ref, x_ref, xn_ref, mod_ref, nw_ref, win_ref, ca_ref, cb_ref, cbb_ref,
                  lng_ref, lnb_ref, wout_ref, o_ref,
                  h_scr, p_scr, s_scr, ub_scr, yb_scr, z_scr, *, T, CH):
    shift = mod_ref[0, 0:1, :]
    scale = mod_ref[0, 1:2, :]
    gate = mod_ref[0, 2:3, :]
    halo = 16
    _fill_h(h_scr, xp_ref, x_ref, xn_ref, nw_ref[...], shift, scale, T, halo)

    n_chunks = win_ref.shape[0]
    ncb = CH // LANES
    rows = T + 2 * halo
    rb = 32

    for c in range(n_chunks):
        p = jnp.dot(h_scr[...], win_ref[c], preferred_element_type=F32)
        for j in range(5 * ncb):
            pj = p[:, j * LANES:(j + 1) * LANES]
            p_scr[c, j, halo:halo + T, :] = pj[0:T]
            p_scr[c, j, 0:halo, :] = pj[T:T + halo]
            p_scr[c, j, halo + T:rows, :] = pj[T + halo:rows]

    for c in range(n_chunks):
        for jj in range(ncb):
            cb = c * ncb + jj
            for r in range(rows // rb):
                sl = slice(r * rb, (r + 1) * rb)
                s_scr[cb, sl, :] = p_scr[c, 1 * ncb + jj, sl, :] * p_scr[c, 2 * ncb + jj, sl, :]
                ub_scr[cb, sl, :] = p_scr[c, 3 * ncb + jj, sl, :] / (1.0 + jnp.exp(-p_scr[c, 4 * ncb + jj, sl, :]))

    a_width = n_chunks * CH
    for c in range(n_chunks):
        for jj in range(ncb):
            cb = c * ncb + jj
            lanes = slice(cb * LANES, (cb + 1) * LANES)
            wa = ca_ref[:, lanes]
            wb = cb_ref[:, lanes]
            bias = cbb_ref[:, lanes]
            for r in range(T // rb):
                r0 = r * rb
                conv_a = (wa[0:1] * s_scr[cb, halo - 1 + r0:halo - 1 + r0 + rb, :]
                          + wa[1:2] * s_scr[cb, halo + r0:halo + r0 + rb, :]
                          + wa[2:3] * s_scr[cb, halo + 1 + r0:halo + 1 + r0 + rb, :])
                ya = p_scr[c, jj, halo + r0:halo + r0 + rb, :] * conv_a
                z_scr[r0:r0 + rb, lanes] = ya.astype(BF16)
                acc = bias + wb[0:1] * ub_scr[cb, 1 + r0:1 + r0 + rb, :]
                for k in range(1, B_CONV):
                    acc = acc + wb[k:k + 1] * ub_scr[cb, 1 + k + r0:1 + k + r0 + rb, :]
                yb_scr[r0:r0 + rb, lanes] = acc

    for r in range(T // rb):
        sl = slice(r * rb, (r + 1) * rb)
        v = yb_scr[sl, :]
        mu = jnp.mean(v, axis=-1, keepdims=True)
        dv = v - mu
        var = jnp.mean(dv * dv, axis=-1, keepdims=True)
        y = dv * lax.rsqrt(var + EPS) * lng_ref[...] + lnb_ref[...]
        z_scr[sl, a_width:] = _silu(y).astype(BF16)

    y = jnp.dot(z_scr[...], wout_ref[...], preferred_element_type=F32)
    rbo = 64
    for r in range(T // rbo):
        sl = slice(r * rbo, (r + 1) * rbo)
        o_ref[0, sl, :] = x_ref[0, sl, :] + gate * y[sl, :]


def _mixer(x, mod, norm_w, win, ca, cb, cbb, lng, lnb, wout, T):
    bsz, seq, d = x.shape
    n_chunks, _, c5 = win.shape
    CH = c5 // 5
    width = n_chunks * CH
    nT = seq // T
    halo = 16
    in_specs = _halo_specs(T, halo, d, nT) + [
        _mod_spec(mod, d),
        _const_spec((1, d)),
        _const_spec((n_chunks, d, 5 * CH)),
        _const_spec((A_CONV, width)),
        _const_spec((B_CONV, width)),
        _const_spec((1, width)),
        _const_spec((1, width)),
        _const_spec((1, width)),
        _const_spec((2 * width, d)),
    ]
    ncb_total = width // LANES
    return pl.pallas_call(
        functools.partial(_mixer_kernel, T=T, CH=CH),
        grid=(bsz, nT),
        in_specs=in_specs,
        out_specs=pl.BlockSpec((1, T, d), lambda b, i: (b, i, 0)),
        out_shape=jax.ShapeDtypeStruct((bsz, seq, d), F32),
        scratch_shapes=[
            pltpu.VMEM((T + 2 * halo, d), BF16),
            pltpu.VMEM((n_chunks, 5 * CH // LANES, T + 2 * halo, LANES), F32),
            pltpu.VMEM((ncb_total, T + 2 * halo, LANES), F32),
            pltpu.VMEM((ncb_total, T + 2 * halo, LANES), F32),
            pltpu.VMEM((T, width), F32),
            pltpu.VMEM((T, 2 * width), BF16),
        ],
        compiler_params=pltpu.CompilerParams(
            dimension_semantics=("arbitrary", "arbitrary"), vmem_limit_bytes=VMEM_LIMIT),
        name="conv_mixer",
    )(x, x, x, mod, norm_w.reshape(1, d), win, ca, cb, cbb.reshape(1, width),
      lng.reshape(1, width), lnb.reshape(1, width), wout)


def _qkv_kernel(x_ref, mod_ref, nw_ref, wqk_ref, wvt_ref, *rest, T, q_w, rope):
    if rope:
        cos_ref, sin_ref, q_ref, kd_ref, vt_ref, h_scr = rest
    else:
        q_ref, kd_ref, vt_ref, h_scr = rest
    shift = mod_ref[0, 0:1, :]
    scale = mod_ref[0, 1:2, :]
    gain = nw_ref[...]
    rbn = 32
    for r in range(T // rbn):
        sl = slice(r * rbn, (r + 1) * rbn)
        h_scr[sl, :] = _norm_mod(x_ref[0, sl, :], gain, shift, scale).astype(BF16)
    h = h_scr[...]
    qk = jnp.dot(h, wqk_ref[...], preferred_element_type=F32)
    vt = lax.dot_general(wvt_ref[...], h, NT_DIMS, preferred_element_type=F32)
    vt_ref[0] = vt.astype(BF16)

    rb = 64
    lane = lax.broadcasted_iota(jnp.int32, (rb, LANES), 1)
    first_half = (lane % HEAD_DIM) < (HEAD_DIM // 2)
    q_scale = HEAD_DIM ** -0.5
    for j in range(qk.shape[1] // LANES):
        for r in range(T // rb):
            sl = slice(r * rb, (r + 1) * rb)
            blk = qk[sl, j * LANES:(j + 1) * LANES]
            if rope:
                partner = jnp.where(first_half,
                                    pltpu.roll(blk, LANES - HEAD_DIM // 2, 1),
                                    pltpu.roll(blk, HEAD_DIM // 2, 1))
                blk = blk * cos_ref[sl, :] + partner * sin_ref[sl, :]
            if j * LANES < q_w:
                q_ref[0, sl, j * LANES:(j + 1) * LANES] = (blk * q_scale).astype(BF16)
            else:
                kd_ref[0, sl, j * LANES - q_w:(j + 1) * LANES - q_w] = blk.astype(BF16)


def _qkv(x, mod, norm_w, wqk, wvt, tables, T):
    bsz, seq, d = x.shape
    kv_w = wvt.shape[0]
    qk_w = wqk.shape[1]
    q_w = qk_w - 2 * kv_w
    nT = seq // T
    rope = tables is not None
    in_specs = [
        pl.BlockSpec((1, T, d), lambda b, i: (b, i, 0)),
        _mod_spec(mod, d),
        _const_spec((1, d)),
        _const_spec((d, qk_w)),
        _const_spec((kv_w, d)),
    ]
    args = [x, mod, norm_w.reshape(1, d), wqk, wvt]
    if rope:
        in_specs += [pl.BlockSpec((T, LANES), lambda b, i: (i, 0))] * 2
        args += list(tables)
    return pl.pallas_call(
        functools.partial(_qkv_kernel, T=T, q_w=q_w, rope=rope),
        grid=(bsz, nT),
        in_specs=in_specs,
        out_specs=[
            pl.BlockSpec((1, T, q_w), lambda b, i: (b, i, 0)),
            pl.BlockSpec((1, T, 2 * kv_w), lambda b, i: (b, i, 0)),
            pl.BlockSpec((1, kv_w, T), lambda b, i: (b, 0, i)),
        ],
        out_shape=[
            jax.ShapeDtypeStruct((bsz, seq, q_w), BF16),
            jax.ShapeDtypeStruct((bsz, seq, 2 * kv_w), BF16),
            jax.ShapeDtypeStruct((bsz, kv_w, seq), BF16),
        ],
        scratch_shapes=[pltpu.VMEM((T, d), BF16)],
        compiler_params=pltpu.CompilerParams(
            dimension_semantics=("arbitrary", "arbitrary"), vmem_limit_bytes=VMEM_LIMIT),
        name="qkv_rope",
    )(*args)


def _rope_tables(length):
    rows = length // GRID_W
    row = jnp.repeat(jnp.arange(rows), GRID_W).astype(F32)
    col = jnp.tile(jnp.arange(GRID_W), rows).astype(F32)
    n_freq = HEAD_DIM // 4
    inv_freq = ROPE_THETA ** (-jnp.arange(n_freq, dtype=F32) / n_freq)
    ang = jnp.concatenate([row[:, None] * inv_freq, col[:, None] * inv_freq], axis=-1)
    cos, sin = jnp.cos(ang), jnp.sin(ang)
    reps = LANES // HEAD_DIM
    return (jnp.tile(jnp.concatenate([cos, cos], axis=-1), (1, reps)),
            jnp.tile(jnp.concatenate([-sin, sin], axis=-1), (1, reps)))


def _attn_kernel(x_ref, mod_ref, q_ref, *rest, TQ, n_ctx, local):
    if local:
        (kdp_ref, kdm_ref, kdn_ref, vtp_ref, vtm_ref, vtn_ref, kdc_ref, vtc_ref, wo_ref, sink_ref,
         o_ref, klo_scr, khi_scr, vcat_scr, ot_scr) = rest
    else:
        kdc_ref, vtc_ref, wo_ref, sink_ref, o_ref, klo_scr, khi_scr, vcat_scr, ot_scr = rest
    i = pl.program_id(1)
    nq = TQ // BLOCK
    n_blocks = pl.num_programs(1) * nq
    gate = mod_ref[0, 2:3, :]
    kd_w = kdc_ref.shape[2]

    def put_keys(k, r0):
        lane = lax.broadcasted_iota(jnp.int32, k.shape, 1)
        low = (lane % LANES) < HEAD_DIM
        zero = jnp.zeros_like(k)
        klo_scr[r0:r0 + k.shape[0], :] = jnp.where(low, k, zero)
        khi_scr[r0:r0 + k.shape[0], :] = jnp.where(low, zero, k)

    put_keys(kdc_ref[0], 0)
    if local:
        put_keys(kdp_ref[0], n_ctx)
        put_keys(kdm_ref[0], n_ctx + BLOCK)
        put_keys(kdn_ref[0], n_ctx + BLOCK + TQ)
        vcat_scr[:, 0:BLOCK] = vtp_ref[0]
        vcat_scr[:, BLOCK:BLOCK + TQ] = vtm_ref[0]
        vcat_scr[:, BLOCK + TQ:2 * BLOCK + TQ] = vtn_ref[0]

    n_loc = 3 * BLOCK
    for n in range(nq):
        qs = slice(n * BLOCK, (n + 1) * BLOCK)
        if local:
            kk = lax.broadcasted_iota(jnp.int32, (n_loc, BLOCK), 0)
            qq = lax.broadcasted_iota(jnp.int32, (n_loc, BLOCK), 1)
            gb = i * nq + n
            lo = jnp.maximum(qq, jnp.where(gb > 0, 0, BLOCK))
            hi = jnp.minimum(qq + 2 * BLOCK, jnp.where(gb < n_blocks - 1, n_loc - 1, 2 * BLOCK - 1))
            mask = (kk >= lo) & (kk <= hi)
        for h in range(N_KV_HEADS):
            hl = slice(h * LANES, (h + 1) * LANES)
            qcol = h * Q_GROUP * HEAD_DIM
            q2 = jnp.concatenate([q_ref[0, qs, qcol:qcol + LANES],
                                  q_ref[0, qs, qcol + LANES:qcol + 2 * LANES]], axis=0)
            s_ctx = [lax.dot_general(ks[0:n_ctx, hl], q2, NT_DIMS, preferred_element_type=F32)
                     for ks in (klo_scr, khi_scr)]
            if local:
                r0 = n_ctx + n * BLOCK
                s_loc = [lax.dot_general(ks[r0:r0 + n_loc, hl], q2, NT_DIMS, preferred_element_type=F32)
                         for ks in (klo_scr, khi_scr)]
            p_ctx, p_loc, denom = [], [], []
            for g in range(Q_GROUP):
                cols = slice((g // 2) * BLOCK, (g // 2 + 1) * BLOCK)
                sink = sink_ref[h * Q_GROUP + g]
                sc = s_ctx[g % 2][:, cols]
                m = jnp.max(sc, axis=0, keepdims=True)
                if local:
                    sl = jnp.where(mask, s_loc[g % 2][:, cols], NEG_INF)
                    m = jnp.maximum(m, jnp.max(sl, axis=0, keepdims=True))
                m = jnp.maximum(m, sink)
                pc = jnp.exp(sc - m)
                den = jnp.sum(pc, axis=0, keepdims=True) + jnp.exp(sink - m)
                if local:
                    pw = jnp.exp(sl - m)
                    den = den + jnp.sum(pw, axis=0, keepdims=True)
                    p_loc.append(pw.astype(BF16))
                p_ctx.append(pc.astype(BF16))
                denom.append(den)
            vrows = slice(h * HEAD_DIM, (h + 1) * HEAD_DIM)
            ot = jnp.dot(vtc_ref[0, vrows, :], jnp.concatenate(p_ctx, axis=1),
                         preferred_element_type=F32)
            if local:
                ot = ot + jnp.dot(vcat_scr[vrows, n * BLOCK:n * BLOCK + n_loc],
                                  jnp.concatenate(p_loc, axis=1), preferred_element_type=F32)
            ot = ot / jnp.concatenate(denom, axis=1)
            for g in range(Q_GROUP):
                hrow = (h * Q_GROUP + g) * HEAD_DIM
                ot_scr[hrow:hrow + HEAD_DIM, qs] = ot[:, g * BLOCK:(g + 1) * BLOCK].astype(BF16)

    y = lax.dot_general(ot_scr[...], wo_ref[...], TN_DIMS, preferred_element_type=F32)
    rb = 64
    for r in range(TQ // rb):
        sl = slice(r * rb, (r + 1) * rb)
        o_ref[0, sl, :] = x_ref[0, sl, :] + gate * y[sl, :]


def _attention(x, mod, q, kd, vt, kdc, vtc, wo, sinks, TQ, local):
    bsz, seq, d = x.shape
    q_w = q.shape[2]
    kd_w = kdc.shape[2]
    kv_w = vtc.shape[1]
    n_ctx = kdc.shape[1]
    nT = seq // TQ
    per = TQ // BLOCK
    nblk = seq // BLOCK
    in_specs = [
        pl.BlockSpec((1, TQ, d), lambda b, i: (b, i, 0)),
        _mod_spec(mod, d),
        pl.BlockSpec((1, TQ, q_w), lambda b, i: (b, i, 0)),
    ]
    args = [x, mod, q]
    if local:
        in_specs += [
            pl.BlockSpec((1, BLOCK, kd_w), lambda b, i: (b, jnp.maximum(i * per - 1, 0), 0)),
            pl.BlockSpec((1, TQ, kd_w), lambda b, i: (b, i, 0)),
            pl.BlockSpec((1, BLOCK, kd_w), lambda b, i: (b, jnp.minimum((i + 1) * per, nblk - 1), 0)),
            pl.BlockSpec((1, kv_w, BLOCK), lambda b, i: (b, 0, jnp.maximum(i * per - 1, 0))),
            pl.BlockSpec((1, kv_w, TQ), lambda b, i: (b, 0, i)),
            pl.BlockSpec((1, kv_w, BLOCK), lambda b, i: (b, 0, jnp.minimum((i + 1) * per, nblk - 1))),
        ]
        args += [kd, kd, kd, vt, vt, vt]
    in_specs += [
        pl.BlockSpec((1, n_ctx, kd_w), lambda b, i: (b, 0, 0)),
        pl.BlockSpec((1, kv_w, n_ctx), lambda b, i: (b, 0, 0)),
        _const_spec((q_w, d)),
        pl.BlockSpec(memory_space=pltpu.SMEM),
    ]
    args += [kdc, vtc, wo, sinks]
    key_rows = n_ctx + (TQ + 2 * BLOCK if local else 0)
    return pl.pallas_call(
        functools.partial(_attn_kernel, TQ=TQ, n_ctx=n_ctx, local=local),
        grid=(bsz, nT),
        in_specs=in_specs,
        out_specs=pl.BlockSpec((1, TQ, d), lambda b, i: (b, i, 0)),
        out_shape=jax.ShapeDtypeStruct((bsz, seq, d), F32),
        scratch_shapes=[
            pltpu.VMEM((key_rows, kd_w), BF16),
            pltpu.VMEM((key_rows, kd_w), BF16),
            pltpu.VMEM((kv_w, TQ + 2 * BLOCK), BF16),
            pltpu.VMEM((q_w, TQ), BF16),
        ],
        compiler_params=pltpu.CompilerParams(
            dimension_semantics=("arbitrary", "arbitrary"), vmem_limit_bytes=VMEM_LIMIT),
        name="window_attn" if local else "ctx_attn",
    )(*args)


def kernel(x, c, ctx, c_ctx, w_mod, b_mod, norm_mix, norm_ffn, w_in_ab, conv_a, conv_b, conv_b_bias,
           ln_b_gain, ln_b_bias, w_out_ab, w_qkv, w_o, sinks, w_up, w_conv_ffn, w_down, final_norm):
    bsz, seq, d = x.shape
    n_ctx = ctx.shape[1]
    depth = w_mod.shape[0]
    d_ff = w_down.shape[1]
    a_width = conv_a.shape[2]
    q_w = w_o.shape[1]
    kv_w = (w_qkv.shape[2] - q_w) // 2

    T_FFN, T_MIX, T_QKV, TQ = 512, 512, 512, 256
    CW = 256
    CH = 256

    cond = jnp.zeros((16, d), F32).at[:bsz].set(c).at[bsz].set(c_ctx)
    mods = _modulation(cond, w_mod, b_mod).reshape(depth, 16, 6, d)

    tables = _rope_tables(seq)
    xc = ctx
    for l in range(depth):
        last = l == depth - 1
        mod_x = mods[l, :bsz]
        mod_c = mods[l, bsz:bsz + 1]
        if l % 2 == 0:
            e = l // 2
            nch = a_width // CH
            win = (w_in_ab[e].reshape(d, 5, nch, CH).transpose(2, 0, 1, 3)
                   .reshape(nch, d, 5 * CH).astype(BF16))
            wout = w_out_ab[e].astype(BF16)
            margs = (norm_mix[l], win, conv_a[e], conv_b[e], conv_b_bias[e], ln_b_gain[e], ln_b_bias[e], wout)
            x = _mixer(x, mod_x, *margs, T=T_MIX)
            if not last:
                xc = _mixer(xc, mod_c, *margs, T=n_ctx)
        else:
            o = l // 2
            wq = w_qkv[o][:, :q_w]
            wk = w_qkv[o][:, q_w:q_w + kv_w]
            wv = w_qkv[o][:, q_w + kv_w:]
            wk_dup = jnp.broadcast_to(wk.reshape(d, N_KV_HEADS, 1, HEAD_DIM),
                                      (d, N_KV_HEADS, 2, HEAD_DIM)).reshape(d, 2 * kv_w)
            wqk = jnp.concatenate([wq, wk_dup], axis=1).astype(BF16)
            wvt = wv.T.astype(BF16)
            wo = w_o[o].astype(BF16)
            q, kd, vt = _qkv(x, mod_x, norm_mix[l], wqk, wvt, tables, T=T_QKV)
            qc, kdc, vtc = _qkv(xc, mod_c, norm_mix[l], wqk, wvt, None, T=n_ctx)
            x = _attention(x, mod_x, q, kd, vt, kdc, vtc, wo, sinks[o], TQ=TQ, local=True)
            if not last:
                xc = _attention(xc, mod_c, qc, None, None, kdc, vtc, wo, sinks[o], TQ=n_ctx, local=False)
        nch = d_ff // CW
        wup = (w_up[l].reshape(d, 2, nch, CW).transpose(2, 0, 1, 3).reshape(nch, d, 2 * CW).astype(BF16))
        cw = w_conv_ffn[l].reshape(FFN_CONV, 2, nch, CW).transpose(2, 0, 1, 3).reshape(nch, FFN_CONV, 2 * CW)
        wdn = w_down[l].astype(BF16)
        x = _ffn(x, mod_x, norm_ffn[l], wup, cw, wdn, final_norm if last else None, T=T_FFN)
        if not last:
            xc = _ffn(xc, mod_c, norm_ffn[l], wup, cw, wdn, None, T=n_ctx)
    return x
```

```python
import functools

import jax
import jax.numpy as jnp
from jax import lax
from jax.experimental import pallas as pl
from jax.experimental.pallas import tpu as pltpu

F32 = jnp.float32
BF16 = jnp.bfloat16

EPS = 1e-6
NEG_INF = -1e30
GRID_W = 64
ROPE_THETA = 10000.0
HEAD_DIM = 64
N_KV_HEADS = 4
Q_GROUP = 4
BLOCK = 128
A_CONV = 3
B_CONV = 31
FFN_CONV = 3

LOG2E = 1.4426950408889634
V_ROWS = 80

LANES = 128
VMEM_LIMIT = 56 * 1024 * 1024

NT_DIMS = (((1,), (1,)), ((), ()))
TN_DIMS = (((0,), (0,)), ((), ()))


def _const_spec(shape):
    return pl.BlockSpec(shape, lambda *_: (0,) * len(shape), pipeline_mode=pl.Buffered(1))


def _layer_spec(arr, layer):
    tail = arr.shape[1:]
    return pl.BlockSpec((1,) + tail, lambda *_: (layer,) + (0,) * len(tail), pipeline_mode=pl.Buffered(1))


def _norm_mod(x, gain, shift, scale):
    ms = jnp.mean(x * x, axis=-1, keepdims=True)
    y = x * lax.rsqrt(ms + EPS) * gain
    return y * (1.0 + scale) + shift


def _silu(x):
    return x / (1.0 + jnp.exp(-x))


def _mod_kernel(c_ref, w_ref, b_ref, o_ref):
    s = _silu(c_ref[...]).astype(BF16)
    o_ref[0] = jnp.dot(s, w_ref[0].astype(BF16), preferred_element_type=F32) + b_ref[0]


def _modulation(cond, w_mod, b_mod):
    depth, d, n = w_mod.shape
    nb = n // 4
    rows = cond.shape[0]
    return pl.pallas_call(
        _mod_kernel,
        grid=(depth, n // nb),
        in_specs=[
            pl.BlockSpec((rows, d), lambda l, j: (0, 0)),
            pl.BlockSpec((1, d, nb), lambda l, j: (l, 0, j)),
            pl.BlockSpec((1, 1, nb), lambda l, j: (l, 0, j)),
        ],
        out_specs=pl.BlockSpec((1, rows, nb), lambda l, j: (l, 0, j)),
        out_shape=jax.ShapeDtypeStruct((depth, rows, n), F32),
        compiler_params=pltpu.CompilerParams(
            dimension_semantics=("arbitrary", "arbitrary"), vmem_limit_bytes=VMEM_LIMIT),
        name="adaln_mod",
    )(cond, w_mod, b_mod.reshape(depth, 1, n))


def _fill_h(h_scr, xp_ref, x_ref, xn_ref, gain, shift, scale, T, halo):
    i = pl.program_id(1)
    last = pl.num_programs(1) - 1
    rb = 32
    for r in range(T // rb):
        h_scr[r * rb:(r + 1) * rb, :] = _norm_mod(
            x_ref[0, r * rb:(r + 1) * rb, :], gain, shift, scale).astype(BF16)
    hp = jnp.where(i > 0, _norm_mod(xp_ref[0], gain, shift, scale), 0.0)
    hn = jnp.where(i < last, _norm_mod(xn_ref[0], gain, shift, scale), 0.0)
    if halo == 8:
        h_scr[T:T + 16, :] = jnp.concatenate([hp, hn], axis=0).astype(BF16)
    else:
        h_scr[T:T + halo, :] = hp.astype(BF16)
        h_scr[T + halo:T + 2 * halo, :] = hn.astype(BF16)


def _halo_specs(T, halo, d, n_seq_blocks):
    per = T // halo
    return [
        pl.BlockSpec((1, halo, d), lambda b, i: (b, jnp.maximum(i * per - 1, 0), 0)),
        pl.BlockSpec((1, T, d), lambda b, i: (b, i, 0)),
        pl.BlockSpec((1, halo, d), lambda b, i: (b, jnp.minimum((i + 1) * per, n_seq_blocks * per - 1), 0)),
    ]


def _mod_spec(mod, d):
    if mod.shape[0] == 1:
        return pl.BlockSpec((1, 6, d), lambda b, i: (0, 0, 0))
    return pl.BlockSpec((1, 6, d), lambda b, i: (b, 0, 0))


def _ffn_kernel(xp_ref, x_ref, xn_ref, mod_ref, nw_ref, wup_ref, cw_ref, wdn_ref, *rest,
                T, n_chunks, CW, final):
    if final:
        fw_ref, o_ref, h_scr, u_scr, act_scr = rest
    else:
        o_ref, h_scr, u_scr, act_scr = rest
    shift = mod_ref[0, 3:4, :]
    scale = mod_ref[0, 4:5, :]
    gate = mod_ref[0, 5:6, :]
    _fill_h(h_scr, xp_ref, x_ref, xn_ref, nw_ref[0], shift, scale, T, 8)

    ncb = CW // LANES
    d_ff = n_chunks * CW
    rb = 64

    def up(c):
        slot = c % 2
        for half in range(2):
            col = half * d_ff + c * CW
            u = jnp.dot(h_scr[...], wup_ref[0, :, col:col + CW], preferred_element_type=F32)
            for j in range(ncb):
                uj = u[:, j * LANES:(j + 1) * LANES]
                u_scr[slot, half * ncb + j, 8:8 + T, :] = uj[0:T]
                u_scr[slot, half * ncb + j, 0:8, :] = uj[T:T + 8]
                u_scr[slot, half * ncb + j, 8 + T:16 + T, :] = uj[T + 8:T + 16]

    def convact(c):
        slot = c % 2
        for j in range(ncb):
            col = c * CW + j * LANES
            wa = cw_ref[0, :, col:col + LANES]
            wg = cw_ref[0, :, d_ff + col:d_ff + col + LANES]
            for r in range(T // rb):
                r0 = r * rb

                def conv(jj, w):
                    return (w[0:1] * u_scr[slot, jj, 7 + r0:7 + r0 + rb, :]
                            + w[1:2] * u_scr[slot, jj, 8 + r0:8 + r0 + rb, :]
                            + w[2:3] * u_scr[slot, jj, 9 + r0:9 + r0 + rb, :])

                a = conv(j, wa)
                g = conv(ncb + j, wg)
                act_scr[r0:r0 + rb, c * CW + j * LANES:c * CW + (j + 1) * LANES] = (_silu(g) * a).astype(BF16)

    up(0)
    for c in range(n_chunks):
        if c + 1 < n_chunks:
            up(c + 1)
        convact(c)

    y = jnp.dot(act_scr[...], wdn_ref[0], preferred_element_type=F32)
    for r in range(T // rb):
        out = x_ref[0, r * rb:(r + 1) * rb, :] + gate * y[r * rb:(r + 1) * rb, :]
        if final:
            ms = jnp.mean(out * out, axis=-1, keepdims=True)
            out = out * lax.rsqrt(ms + EPS) * fw_ref[...]
        o_ref[0, r * rb:(r + 1) * rb, :] = out


def _ffn(x, mod, layer, norm_w, wup, cw, wdn, final_w, T, CW):
    bsz, seq, d = x.shape
    d_ff = wdn.shape[1]
    n_chunks = d_ff // CW
    nT = seq // T
    final = final_w is not None
    in_specs = _halo_specs(T, 8, d, nT) + [
        _mod_spec(mod, d),
        _layer_spec(norm_w, layer),
        _layer_spec(wup, layer),
        _layer_spec(cw, layer),
        _layer_spec(wdn, layer),
    ]
    args = [x, x, x, mod, norm_w, wup, cw, wdn]
    if final:
        in_specs.append(_const_spec((1, d)))
        args.append(final_w.reshape(1, d))
    return pl.pallas_call(
        functools.partial(_ffn_kernel, T=T, n_chunks=n_chunks, CW=CW, final=final),
        grid=(bsz, nT),
        in_specs=in_specs,
        out_specs=pl.BlockSpec((1, T, d), lambda b, i: (b, i, 0)),
        out_shape=jax.ShapeDtypeStruct((bsz, seq, d), F32),
        scratch_shapes=[
            pltpu.VMEM((T + 16, d), BF16),
            pltpu.VMEM((2, 2 * CW // LANES, T + 16, LANES), F32),
            pltpu.VMEM((T, d_ff), BF16),
        ],
        compiler_params=pltpu.CompilerParams(
            dimension_semantics=("arbitrary", "arbitrary"), vmem_limit_bytes=VMEM_LIMIT),
        name="conv_ffn",
    )(*args)


def _mixer_kernel(xp_ref, x_ref, xn_ref, mod_ref, nw_ref, win_ref, ca_ref, cb_ref, cbb_ref,
                  lng_ref, lnb_ref, wout_ref, o_ref,
                  h_scr, p_scr, s_scr, ub_scr, yb_scr, z_scr, *, T, CH):
    shift = mod_ref[0, 0:1, :]
    scale = mod_ref[0, 1:2, :]
    gate = mod_ref[0, 2:3, :]
    halo = 16
    _fill_h(h_scr, xp_ref, x_ref, xn_ref, nw_ref[0], shift, scale, T, halo)

    a_width = ca_ref.shape[2]
    n_chunks = a_width // CH
    ncb = CH // LANES
    rows = T + 2 * halo
    rb = 32

    for c in range(n_chunks):
        for k in range(5):
            col = k * a_width + c * CH
            p = jnp.dot(h_scr[...], win_ref[0, :, col:col + CH], preferred_element_type=F32)
            for jj in range(ncb):
                pj = p[:, jj * LANES:(jj + 1) * LANES]
                p_scr[c, k * ncb + jj, halo:halo + T, :] = pj[0:T]
                p_scr[c, k * ncb + jj, 0:halo, :] = pj[T:T + halo]
                p_scr[c, k * ncb + jj, halo + T:rows, :] = pj[T + halo:rows]

    for c in range(n_chunks):
        for jj in range(ncb):
            cb = c * ncb + jj
            for r in range(rows // rb):
                sl = slice(r * rb, (r + 1) * rb)
                s_scr[cb, sl, :] = p_scr[c, 1 * ncb + jj, sl, :] * p_scr[c, 2 * ncb + jj, sl, :]
                ub_scr[cb, sl, :] = p_scr[c, 3 * ncb + jj, sl, :] / (1.0 + jnp.exp(-p_scr[c, 4 * ncb + jj, sl, :]))

    for c in range(n_chunks):
        for jj in range(ncb):
            cb = c * ncb + jj
            lanes = slice(cb * LANES, (cb + 1) * LANES)
            wa = ca_ref[0, :, lanes]
            wb = cb_ref[0, :, lanes]
            bias = cbb_ref[0, :, lanes]
            for r in range(T // rb):
                r0 = r * rb
                conv_a = (wa[0:1] * s_scr[cb, halo - 1 + r0:halo - 1 + r0 + rb, :]
                          + wa[1:2] * s_scr[cb, halo + r0:halo + r0 + rb, :]
                          + wa[2:3] * s_scr[cb, halo + 1 + r0:halo + 1 + r0 + rb, :])
                ya = p_scr[c, jj, halo + r0:halo + r0 + rb, :] * conv_a
                z_scr[r0:r0 + rb, lanes] = ya.astype(BF16)
                acc = bias + wb[0:1] * ub_scr[cb, 1 + r0:1 + r0 + rb, :]
                for k in range(1, B_CONV):
                    acc = acc + wb[k:k + 1] * ub_scr[cb, 1 + k + r0:1 + k + r0 + rb, :]
                yb_scr[r0:r0 + rb, lanes] = acc

    for r in range(T // rb):
        sl = slice(r * rb, (r + 1) * rb)
        v = yb_scr[sl, :]
        mu = jnp.mean(v, axis=-1, keepdims=True)
        dv = v - mu
        var = jnp.mean(dv * dv, axis=-1, keepdims=True)
        y = dv * lax.rsqrt(var + EPS) * lng_ref[0] + lnb_ref[0]
        z_scr[sl, a_width:] = _silu(y).astype(BF16)

    y = jnp.dot(z_scr[...], wout_ref[0], preferred_element_type=F32)
    rbo = 64
    for r in range(T // rbo):
        sl = slice(r * rbo, (r + 1) * rbo)
        o_ref[0, sl, :] = x_ref[0, sl, :] + gate * y[sl, :]


def _mixer(x, mod, layer, e, norm_w, win, ca, cb, cbb, lng, lnb, wout, T, CH):
    bsz, seq, d = x.shape
    width = ca.shape[2]
    n_chunks = width // CH
    nT = seq // T
    halo = 16
    in_specs = _halo_specs(T, halo, d, nT) + [
        _mod_spec(mod, d),
        _layer_spec(norm_w, layer),
        _layer_spec(win, e),
        _layer_spec(ca, e),
        _layer_spec(cb, e),
        _layer_spec(cbb, e),
        _layer_spec(lng, e),
        _layer_spec(lnb, e),
        _layer_spec(wout, e),
    ]
    ncb_total = width // LANES
    return pl.pallas_call(
        functools.partial(_mixer_kernel, T=T, CH=CH),
        grid=(bsz, nT),
        in_specs=in_specs,
        out_specs=pl.BlockSpec((1, T, d), lambda b, i: (b, i, 0)),
        out_shape=jax.ShapeDtypeStruct((bsz, seq, d), F32),
        scratch_shapes=[
            pltpu.VMEM((T + 2 * halo, d), BF16),
            pltpu.VMEM((n_chunks, 5 * CH // LANES, T + 2 * halo, LANES), F32),
            pltpu.VMEM((ncb_total, T + 2 * halo, LANES), F32),
            pltpu.VMEM((ncb_total, T + 2 * halo, LANES), F32),
            pltpu.VMEM((T, width), F32),
            pltpu.VMEM((T, 2 * width), BF16),
        ],
        compiler_params=pltpu.CompilerParams(
            dimension_semantics=("arbitrary", "arbitrary"), vmem_limit_bytes=VMEM_LIMIT),
        name="conv_mixer",
    )(x, x, x, mod, norm_w, win, ca, cb, cbb, lng, lnb, wout)


def _qkv_kernel(x_ref, mod_ref, nw_ref, wqk_ref, wvt_ref, *rest, T, q_w, rope):
    if rope:
        cos_ref, sin_ref, q_ref, kd_ref, vt_ref, h_scr = rest
    else:
        q_ref, kd_ref, vt_ref, h_scr = rest
    shift = mod_ref[0, 0:1, :]
    scale = mod_ref[0, 1:2, :]
    gain = nw_ref[0]
    rbn = 32
    for r in range(T // rbn):
        sl = slice(r * rbn, (r + 1) * rbn)
        h_scr[sl, :] = _norm_mod(x_ref[0, sl, :], gain, shift, scale).astype(BF16)
    h = h_scr[...]
    qk = jnp.dot(h, wqk_ref[...], preferred_element_type=F32)
    vt = lax.dot_general(wvt_ref[...], h, NT_DIMS, preferred_element_type=F32)
    ones = jnp.ones((V_ROWS - HEAD_DIM, T), BF16)
    for hh in range(vt.shape[0] // HEAD_DIM):
        vt_ref[0, hh * V_ROWS:hh * V_ROWS + HEAD_DIM, :] = vt[hh * HEAD_DIM:(hh + 1) * HEAD_DIM, :].astype(BF16)
        vt_ref[0, hh * V_ROWS + HEAD_DIM:(hh + 1) * V_ROWS, :] = ones

    rb = 64
    lane = lax.broadcasted_iota(jnp.int32, (rb, LANES), 1)
    first_half = (lane % HEAD_DIM) < (HEAD_DIM // 2)
    q_scale = HEAD_DIM ** -0.5 * LOG2E
    for j in range(qk.shape[1] // LANES):
        for r in range(T // rb):
            sl = slice(r * rb, (r + 1) * rb)
            blk = qk[sl, j * LANES:(j + 1) * LANES]
            if rope:
                partner = jnp.where(first_half,
                                    pltpu.roll(blk, LANES - HEAD_DIM // 2, 1),
                                    pltpu.roll(blk, HEAD_DIM // 2, 1))
                blk = blk * cos_ref[sl, :] + partner * sin_ref[sl, :]
            if j * LANES < q_w:
                q_ref[0, sl, j * LANES:(j + 1) * LANES] = (blk * q_scale).astype(BF16)
            else:
                kd_ref[0, sl, j * LANES - q_w:(j + 1) * LANES - q_w] = blk.astype(BF16)


def _qkv(x, mod, layer, norm_w, wqk, wvt, tables, T):
    bsz, seq, d = x.shape
    kv_w = wvt.shape[0]
    qk_w = wqk.shape[1]
    q_w = qk_w - 2 * kv_w
    nT = seq // T
    vt_rows = kv_w // HEAD_DIM * V_ROWS
    rope = tables is not None
    in_specs = [
        pl.BlockSpec((1, T, d), lambda b, i: (b, i, 0)),
        _mod_spec(mod, d),
        _layer_spec(norm_w, layer),
        _const_spec((d, qk_w)),
        _const_spec((kv_w, d)),
    ]
    args = [x, mod, norm_w, wqk, wvt]
    if rope:
        in_specs += [pl.BlockSpec((T, LANES), lambda b, i: (i, 0))] * 2
        args += list(tables)
    return pl.pallas_call(
        functools.partial(_qkv_kernel, T=T, q_w=q_w, rope=rope),
        grid=(bsz, nT),
        in_specs=in_specs,
        out_specs=[
            pl.BlockSpec((1, T, q_w), lambda b, i: (b, i, 0)),
            pl.BlockSpec((1, T, 2 * kv_w), lambda b, i: (b, i, 0)),
            pl.BlockSpec((1, vt_rows, T), lambda b, i: (b, 0, i)),
        ],
        out_shape=[
            jax.ShapeDtypeStruct((bsz, seq, q_w), BF16),
            jax.ShapeDtypeStruct((bsz, seq, 2 * kv_w), BF16),
            jax.ShapeDtypeStruct((bsz, vt_rows, seq), BF16),
        ],
        scratch_shapes=[pltpu.VMEM((T, d), BF16)],
        compiler_params=pltpu.CompilerParams(
            dimension_semantics=("arbitrary", "arbitrary"), vmem_limit_bytes=VMEM_LIMIT),
        name="qkv_rope",
    )(*args)


def _rope_tables(length):
    rows = length // GRID_W
    row = jnp.repeat(jnp.arange(rows), GRID_W).astype(F32)
    col = jnp.tile(jnp.arange(GRID_W), rows).astype(F32)
    n_freq = HEAD_DIM // 4
    inv_freq = ROPE_THETA ** (-jnp.arange(n_freq, dtype=F32) / n_freq)
    ang = jnp.concatenate([row[:, None] * inv_freq, col[:, None] * inv_freq], axis=-1)
    cos, sin = jnp.cos(ang), jnp.sin(ang)
    reps = LANES // HEAD_DIM
    return (jnp.tile(jnp.concatenate([cos, cos], axis=-1), (1, reps)),
            jnp.tile(jnp.concatenate([-sin, sin], axis=-1), (1, reps)))


def _attn_kernel(x_ref, mod_ref, q_ref, *rest, TQ, n_ctx, local):
    if local:
        (kdp_ref, kdm_ref, kdn_ref, vtp_ref, vtm_ref, vtn_ref, kdc_ref, vtc_ref, wo_ref, sink_ref,
         o_ref, klo_scr, khi_scr, vcat_scr, ot_scr) = rest
    else:
        kdc_ref, vtc_ref, wo_ref, sink_ref, o_ref, klo_scr, khi_scr, vcat_scr, ot_scr = rest
    i = pl.program_id(1)
    nq = TQ // BLOCK
    n_blocks = pl.num_programs(1) * nq
    gate = mod_ref[0, 2:3, :]
    kd_w = kdc_ref.shape[2]

    def put_keys(k, r0):
        lane = lax.broadcasted_iota(jnp.int32, k.shape, 1)
        low = (lane % LANES) < HEAD_DIM
        zero = jnp.zeros_like(k)
        klo_scr[r0:r0 + k.shape[0], :] = jnp.where(low, k, zero)
        khi_scr[r0:r0 + k.shape[0], :] = jnp.where(low, zero, k)

    put_keys(kdc_ref[0], 0)
    if local:
        put_keys(kdp_ref[0], n_ctx)
        put_keys(kdm_ref[0], n_ctx + BLOCK)
        put_keys(kdn_ref[0], n_ctx + BLOCK + TQ)
        vcat_scr[:, 0:BLOCK] = vtp_ref[0]
        vcat_scr[:, BLOCK:BLOCK + TQ] = vtm_ref[0]
        vcat_scr[:, BLOCK + TQ:2 * BLOCK + TQ] = vtn_ref[0]

    n_loc = 3 * BLOCK
    edge = (BLOCK, BLOCK)
    kk = lax.broadcasted_iota(jnp.int32, edge, 0)
    qq = lax.broadcasted_iota(jnp.int32, edge, 1)

    def scores(n, h):
        hl = slice(h * LANES, (h + 1) * LANES)
        qs = slice(n * BLOCK, (n + 1) * BLOCK)
        qcol = h * Q_GROUP * HEAD_DIM
        q2 = jnp.concatenate([q_ref[0, qs, qcol:qcol + LANES],
                              q_ref[0, qs, qcol + LANES:qcol + 2 * LANES]], axis=0)
        out = []
        for ks in (klo_scr, khi_scr):
            sc = lax.dot_general(ks[0:n_ctx, hl], q2, NT_DIMS, preferred_element_type=F32)
            sl = None
            if local:
                r0 = n_ctx + n * BLOCK
                sl = lax.dot_general(ks[r0:r0 + n_loc, hl], q2, NT_DIMS, preferred_element_type=F32)
            out.append((sc, sl))
        return out

    def finish(n, h, s):
        qs = slice(n * BLOCK, (n + 1) * BLOCK)
        if local:
            gb = i * nq + n
            keep_prev = kk >= qq + jnp.where(gb > 0, 0, BLOCK)
            keep_next = kk <= qq - jnp.where(gb < n_blocks - 1, 0, BLOCK)
        p_ctx, p_loc, sink_term = [], [], []
        for g in range(Q_GROUP):
            cols = slice((g // 2) * BLOCK, (g // 2 + 1) * BLOCK)
            sink = jnp.full((1, BLOCK), sink_ref[h * Q_GROUP + g], F32) * LOG2E
            sc, sl = s[g % 2]
            sc = sc[:, cols]
            m = jnp.max(sc, axis=0, keepdims=True)
            if local:
                s_prev = jnp.where(keep_prev, sl[0:BLOCK, cols], NEG_INF)
                s_own = sl[BLOCK:2 * BLOCK, cols]
                s_next = jnp.where(keep_next, sl[2 * BLOCK:n_loc, cols], NEG_INF)
                m = jnp.maximum(m, jnp.max(jnp.maximum(jnp.maximum(s_prev, s_own), s_next),
                                           axis=0, keepdims=True))
            m = jnp.maximum(m, sink)
            p_ctx.append(jnp.exp2(sc - m).astype(BF16))
            if local:
                p_loc.append(jnp.concatenate([jnp.exp2(s_prev - m).astype(BF16),
                                              jnp.exp2(s_own - m).astype(BF16),
                                              jnp.exp2(s_next - m).astype(BF16)], axis=0))
            sink_term.append(jnp.exp2(sink - m))
        vrows = slice(h * V_ROWS, (h + 1) * V_ROWS)
        ot = jnp.dot(vtc_ref[0, vrows, :], jnp.concatenate(p_ctx, axis=1), preferred_element_type=F32)
        if local:
            ot = ot + jnp.dot(vcat_scr[vrows, n * BLOCK:n * BLOCK + n_loc],
                              jnp.concatenate(p_loc, axis=1), preferred_element_type=F32)
        den = ot[HEAD_DIM:HEAD_DIM + 1, :] + jnp.concatenate(sink_term, axis=1)
        ot = ot[0:HEAD_DIM, :] / den
        for g in range(Q_GROUP):
            hrow = (h * Q_GROUP + g) * HEAD_DIM
            ot_scr[hrow:hrow + HEAD_DIM, qs] = ot[:, g * BLOCK:(g + 1) * BLOCK].astype(BF16)

    pairs = [(n, h) for n in range(nq) for h in range(N_KV_HEADS)]
    pending = scores(*pairs[0])
    for k, (n, h) in enumerate(pairs):
        current = pending
        if k + 1 < len(pairs):
            pending = scores(*pairs[k + 1])
        finish(n, h, current)

    y = lax.dot_general(ot_scr[...], wo_ref[...], TN_DIMS, preferred_element_type=F32)
    rb = 64
    for r in range(TQ // rb):
        sl = slice(r * rb, (r + 1) * rb)
        o_ref[0, sl, :] = x_ref[0, sl, :] + gate * y[sl, :]


def _attention(x, mod, q, kd, vt, kdc, vtc, wo, sinks, TQ, local):
    bsz, seq, d = x.shape
    q_w = q.shape[2]
    kd_w = kdc.shape[2]
    vt_rows = vtc.shape[1]
    n_ctx = kdc.shape[1]
    nT = seq // TQ
    per = TQ // BLOCK
    nblk = seq // BLOCK
    in_specs = [
        pl.BlockSpec((1, TQ, d), lambda b, i: (b, i, 0)),
        _mod_spec(mod, d),
        pl.BlockSpec((1, TQ, q_w), lambda b, i: (b, i, 0)),
    ]
    args = [x, mod, q]
    if local:
        in_specs += [
            pl.BlockSpec((1, BLOCK, kd_w), lambda b, i: (b, jnp.maximum(i * per - 1, 0), 0)),
            pl.BlockSpec((1, TQ, kd_w), lambda b, i: (b, i, 0)),
            pl.BlockSpec((1, BLOCK, kd_w), lambda b, i: (b, jnp.minimum((i + 1) * per, nblk - 1), 0)),
            pl.BlockSpec((1, vt_rows, BLOCK), lambda b, i: (b, 0, jnp.maximum(i * per - 1, 0))),
            pl.BlockSpec((1, vt_rows, TQ), lambda b, i: (b, 0, i)),
            pl.BlockSpec((1, vt_rows, BLOCK), lambda b, i: (b, 0, jnp.minimum((i + 1) * per, nblk - 1))),
        ]
        args += [kd, kd, kd, vt, vt, vt]
    in_specs += [
        pl.BlockSpec((1, n_ctx, kd_w), lambda b, i: (b, 0, 0)),
        pl.BlockSpec((1, vt_rows, n_ctx), lambda b, i: (b, 0, 0)),
        _const_spec((q_w, d)),
        pl.BlockSpec(memory_space=pltpu.SMEM),
    ]
    args += [kdc, vtc, wo, sinks]
    key_rows = n_ctx + (TQ + 2 * BLOCK if local else 0)
    return pl.pallas_call(
        functools.partial(_attn_kernel, TQ=TQ, n_ctx=n_ctx, local=local),
        grid=(bsz, nT),
        in_specs=in_specs,
        out_specs=pl.BlockSpec((1, TQ, d), lambda b, i: (b, i, 0)),
        out_shape=jax.ShapeDtypeStruct((bsz, seq, d), F32),
        scratch_shapes=[
            pltpu.VMEM((key_rows, kd_w), BF16),
            pltpu.VMEM((key_rows, kd_w), BF16),
            pltpu.VMEM((vt_rows, TQ + 2 * BLOCK), BF16),
            pltpu.VMEM((q_w, TQ), BF16),
        ],
        compiler_params=pltpu.CompilerParams(
            dimension_semantics=("arbitrary", "arbitrary"), vmem_limit_bytes=VMEM_LIMIT),
        name="window_attn" if local else "ctx_attn",
    )(*args)


def kernel(x, c, ctx, c_ctx, w_mod, b_mod, norm_mix, norm_ffn, w_in_ab, conv_a, conv_b, conv_b_bias,
           ln_b_gain, ln_b_bias, w_out_ab, w_qkv, w_o, sinks, w_up, w_conv_ffn, w_down, final_norm):
    bsz, seq, d = x.shape
    n_ctx = ctx.shape[1]
    depth = w_mod.shape[0]
    d_ff = w_down.shape[1]
    a_width = conv_a.shape[2]
    q_w = w_o.shape[1]
    kv_w = (w_qkv.shape[2] - q_w) // 2

    T_FFN, T_MIX, T_QKV, TQ = 512, 512, 512, 256
    CW = 256
    CH = 256

    cond = jnp.zeros((16, d), F32).at[:bsz].set(c).at[bsz].set(c_ctx)
    mods = _modulation(cond, w_mod, b_mod).reshape(depth, 16, 6, d)

    tables = _rope_tables(seq)
    win_bf, wout_bf = w_in_ab.astype(BF16), w_out_ab.astype(BF16)
    wup_bf, wdn_bf = w_up.astype(BF16), w_down.astype(BF16)
    norm_mix3, norm_ffn3 = norm_mix[:, None, :], norm_ffn[:, None, :]
    margs = (norm_mix3, win_bf, conv_a, conv_b, conv_b_bias[:, None, :], ln_b_gain[:, None, :],
             ln_b_bias[:, None, :], wout_bf)
    fargs = (norm_ffn3, wup_bf, w_conv_ffn, wdn_bf)

    xc = ctx
    for l in range(depth):
        last = l == depth - 1
        mod_x = mods[l, :bsz]
        mod_c = mods[l, bsz:bsz + 1]
        if l % 2 == 0:
            e = l // 2
            x = _mixer(x, mod_x, l, e, *margs, T=T_MIX, CH=CH)
            if not last:
                xc = _mixer(xc, mod_c, l, e, *margs, T=n_ctx, CH=CH)
        else:
            o = l // 2
            wq = w_qkv[o][:, :q_w]
            wk = w_qkv[o][:, q_w:q_w + kv_w]
            wv = w_qkv[o][:, q_w + kv_w:]
            wk_dup = jnp.broadcast_to(wk.reshape(d, N_KV_HEADS, 1, HEAD_DIM),
                                      (d, N_KV_HEADS, 2, HEAD_DIM)).reshape(d, 2 * kv_w)
            wqk = jnp.concatenate([wq, wk_dup], axis=1).astype(BF16)
            wvt = wv.T.astype(BF16)
            wo = w_o[o].astype(BF16)
            q, kd, vt = _qkv(x, mod_x, l, norm_mix3, wqk, wvt, tables, T=T_QKV)
            qc, kdc, vtc = _qkv(xc, mod_c, l, norm_mix3, wqk, wvt, None, T=n_ctx)
            x = _attention(x, mod_x, q, kd, vt, kdc, vtc, wo, sinks[o], TQ=TQ, local=True)
            if not last:
                xc = _attention(xc, mod_c, qc, None, None, kdc, vtc, wo, sinks[o], TQ=n_ctx, local=False)
        x = _ffn(x, mod_x, l, *fargs, final_norm if last else None, T=T_FFN, CW=CW)
        if not last:
            xc = _ffn(xc, mod_c, l, *fargs, None, T=n_ctx, CW=CW)
    return x
```

```python
import functools

import jax
import jax.numpy as jnp
from jax import lax
from jax.experimental import pallas as pl
from jax.experimental.pallas import tpu as pltpu

F32 = jnp.float32
BF16 = jnp.bfloat16

EPS = 1e-6
NEG_INF = -1e30
GRID_W = 64
ROPE_THETA = 10000.0
HEAD_DIM = 64
N_KV_HEADS = 4
Q_GROUP = 4
BLOCK = 128
A_CONV = 3
B_CONV = 31
FFN_CONV = 3

LOG2E = 1.4426950408889634
V_ROWS = 80

LANES = 128
VMEM_LIMIT = 56 * 1024 * 1024

NT_DIMS = (((1,), (1,)), ((), ()))
TN_DIMS = (((0,), (0,)), ((), ()))


def _const_spec(shape):
    return pl.BlockSpec(shape, lambda *_: (0,) * len(shape), pipeline_mode=pl.Buffered(1))


def _layer_spec(arr, layer):
    tail = arr.shape[1:]
    return pl.BlockSpec((1,) + tail, lambda *_: (layer,) + (0,) * len(tail), pipeline_mode=pl.Buffered(1))


def _norm_mod(x, gain, shift, scale):
    ms = jnp.mean(x * x, axis=-1, keepdims=True)
    y = x * lax.rsqrt(ms + EPS) * gain
    return y * (1.0 + scale) + shift


def _silu(x):
    return x / (1.0 + jnp.exp(-x))


def _mod_kernel(c_ref, w_ref, b_ref, o_ref):
    s = _silu(c_ref[...]).astype(BF16)
    o_ref[0] = jnp.dot(s, w_ref[0].astype(BF16), preferred_element_type=F32) + b_ref[0]


def _modulation(cond, w_mod, b_mod):
    depth, d, n = w_mod.shape
    nb = n // 4
    rows = cond.shape[0]
    return pl.pallas_call(
        _mod_kernel,
        grid=(depth, n // nb),
        in_specs=[
            pl.BlockSpec((rows, d), lambda l, j: (0, 0)),
            pl.BlockSpec((1, d, nb), lambda l, j: (l, 0, j)),
            pl.BlockSpec((1, 1, nb), lambda l, j: (l, 0, j)),
        ],
        out_specs=pl.BlockSpec((1, rows, nb), lambda l, j: (l, 0, j)),
        out_shape=jax.ShapeDtypeStruct((depth, rows, n), F32),
        compiler_params=pltpu.CompilerParams(
            dimension_semantics=("arbitrary", "arbitrary"), vmem_limit_bytes=VMEM_LIMIT),
        name="adaln_mod",
    )(cond, w_mod, b_mod.reshape(depth, 1, n))


def _fill_h(h_scr, xp_ref, x_ref, xn_ref, gain, shift, scale, T, halo):
    i = pl.program_id(1)
    last = pl.num_programs(1) - 1
    rb = 32
    for r in range(T // rb):
        h_scr[r * rb:(r + 1) * rb, :] = _norm_mod(
            x_ref[0, r * rb:(r + 1) * rb, :], gain, shift, scale).astype(BF16)
    hp = jnp.where(i > 0, _norm_mod(xp_ref[0], gain, shift, scale), 0.0)
    hn = jnp.where(i < last, _norm_mod(xn_ref[0], gain, shift, scale), 0.0)
    if halo == 8:
        h_scr[T:T + 16, :] = jnp.concatenate([hp, hn], axis=0).astype(BF16)
    else:
        h_scr[T:T + halo, :] = hp.astype(BF16)
        h_scr[T + halo:T + 2 * halo, :] = hn.astype(BF16)


def _halo_specs(T, halo, d, n_seq_blocks):
    per = T // halo
    return [
        pl.BlockSpec((1, halo, d), lambda b, i: (b, jnp.maximum(i * per - 1, 0), 0)),
        pl.BlockSpec((1, T, d), lambda b, i: (b, i, 0)),
        pl.BlockSpec((1, halo, d), lambda b, i: (b, jnp.minimum((i + 1) * per, n_seq_blocks * per - 1), 0)),
    ]


def _mod_spec(mod, d):
    if mod.shape[0] == 1:
        return pl.BlockSpec((1, 6, d), lambda b, i: (0, 0, 0))
    return pl.BlockSpec((1, 6, d), lambda b, i: (b, 0, 0))


def _ffn_kernel(xp_ref, x_ref, xn_ref, mod_ref, nw_ref, wup_ref, cw_ref, wdn_ref, *rest,
                T, n_chunks, CW, final):
    if final:
        fw_ref, o_ref, h_scr, u0, u1, act_scr = rest
    else:
        o_ref, h_scr, u0, u1, act_scr = rest
    u_scr = (u0, u1)
    shift = mod_ref[0, 3:4, :]
    scale = mod_ref[0, 4:5, :]
    gate = mod_ref[0, 5:6, :]
    _fill_h(h_scr, xp_ref, x_ref, xn_ref, nw_ref[0], shift, scale, T, 8)

    ncb = CW // LANES
    d_ff = n_chunks * CW
    rb = 64

    def up(c, half):
        slot = c % 2
        col = half * d_ff + c * CW
        u = jnp.dot(h_scr[...], wup_ref[0, :, col:col + CW], preferred_element_type=F32)
        for j in range(ncb):
            uj = u[:, j * LANES:(j + 1) * LANES]
            u_scr[slot][half * ncb + j, 8:8 + T, :] = uj[0:T]
            u_scr[slot][half * ncb + j, 0:8, :] = uj[T:T + 8]
            u_scr[slot][half * ncb + j, 8 + T:16 + T, :] = uj[T + 8:T + 16]

    def convact(c, j):
        slot = c % 2
        col = c * CW + j * LANES
        wa = cw_ref[0, :, col:col + LANES]
        wg = cw_ref[0, :, d_ff + col:d_ff + col + LANES]
        for r in range(T // rb):
            r0 = r * rb

            def conv(jj, w):
                return (w[0:1] * u_scr[slot][jj, 7 + r0:7 + r0 + rb, :]
                        + w[1:2] * u_scr[slot][jj, 8 + r0:8 + r0 + rb, :]
                        + w[2:3] * u_scr[slot][jj, 9 + r0:9 + r0 + rb, :])

            a = conv(j, wa)
            g = conv(ncb + j, wg)
            act_scr[r0:r0 + rb, col:col + LANES] = (_silu(g) * a).astype(BF16)

    up(0, 0)
    up(0, 1)
    for c in range(n_chunks):
        for half in range(2):
            if c + 1 < n_chunks:
                up(c + 1, half)
            for j in range(half * ncb // 2, (half + 1) * ncb // 2):
                convact(c, j)

    y = jnp.dot(act_scr[...], wdn_ref[0], preferred_element_type=F32)
    for r in range(T // rb):
        out = x_ref[0, r * rb:(r + 1) * rb, :] + gate * y[r * rb:(r + 1) * rb, :]
        if final:
            ms = jnp.mean(out * out, axis=-1, keepdims=True)
            out = out * lax.rsqrt(ms + EPS) * fw_ref[...]
        o_ref[0, r * rb:(r + 1) * rb, :] = out


def _ffn(x, mod, layer, norm_w, wup, cw, wdn, final_w, T, CW):
    bsz, seq, d = x.shape
    d_ff = wdn.shape[1]
    n_chunks = d_ff // CW
    nT = seq // T
    final = final_w is not None
    in_specs = _halo_specs(T, 8, d, nT) + [
        _mod_spec(mod, d),
        _layer_spec(norm_w, layer),
        _layer_spec(wup, layer),
        _layer_spec(cw, layer),
        _layer_spec(wdn, layer),
    ]
    args = [x, x, x, mod, norm_w, wup, cw, wdn]
    if final:
        in_specs.append(_const_spec((1, d)))
        args.append(final_w.reshape(1, d))
    return pl.pallas_call(
        functools.partial(_ffn_kernel, T=T, n_chunks=n_chunks, CW=CW, final=final),
        grid=(bsz, nT),
        in_specs=in_specs,
        out_specs=pl.BlockSpec((1, T, d), lambda b, i: (b, i, 0)),
        out_shape=jax.ShapeDtypeStruct((bsz, seq, d), F32),
        scratch_shapes=[
            pltpu.VMEM((T + 16, d), BF16),
            pltpu.VMEM((2 * CW // LANES, T + 16, LANES), F32),
            pltpu.VMEM((2 * CW // LANES, T + 16, LANES), F32),
            pltpu.VMEM((T, d_ff), BF16),
        ],
        compiler_params=pltpu.CompilerParams(
            dimension_semantics=("arbitrary", "arbitrary"), vmem_limit_bytes=VMEM_LIMIT),
        name="conv_ffn",
    )(*args)


def _mixer_kernel(xp_ref, x_ref, xn_ref, mod_ref, nw_ref, win_ref, ca_ref, cb_ref, cbb_ref,
                  lng_ref, lnb_ref, wout_ref, o_ref,
                  h_scr, p_scr, s_scr, ub_scr, yb_scr, z_scr, *, T, CH):
    shift = mod_ref[0, 0:1, :]
    scale = mod_ref[0, 1:2, :]
    gate = mod_ref[0, 2:3, :]
    halo = 16
    _fill_h(h_scr, xp_ref, x_ref, xn_ref, nw_ref[0], shift, scale, T, halo)

    a_width = ca_ref.shape[2]
    n_chunks = a_width // CH
    ncb = CH // LANES
    rows = T + 2 * halo
    rb = 32

    for c in range(n_chunks):
        for k in range(5):
            col = k * a_width + c * CH
            p = jnp.dot(h_scr[...], win_ref[0, :, col:col + CH], preferred_element_type=F32)
            for jj in range(ncb):
                pj = p[:, jj * LANES:(jj + 1) * LANES]
                p_scr[c, k * ncb + jj, halo:halo + T, :] = pj[0:T]
                p_scr[c, k * ncb + jj, 0:halo, :] = pj[T:T + halo]
                p_scr[c, k * ncb + jj, halo + T:rows, :] = pj[T + halo:rows]

    for c in range(n_chunks):
        for jj in range(ncb):
            cb = c * ncb + jj
            for r in range(rows // rb):
                sl = slice(r * rb, (r + 1) * rb)
                s_scr[cb, sl, :] = p_scr[c, 1 * ncb + jj, sl, :] * p_scr[c, 2 * ncb + jj, sl, :]
                ub_scr[cb, sl, :] = p_scr[c, 3 * ncb + jj, sl, :] / (1.0 + jnp.exp(-p_scr[c, 4 * ncb + jj, sl, :]))

    for c in range(n_chunks):
        for jj in range(ncb):
            cb = c * ncb + jj
            lanes = slice(cb * LANES, (cb + 1) * LANES)
            wa = ca_ref[0, :, lanes]
            wb = cb_ref[0, :, lanes]
            bias = cbb_ref[0, :, lanes]
            for r in range(T // rb):
                r0 = r * rb
                conv_a = (wa[0:1] * s_scr[cb, halo - 1 + r0:halo - 1 + r0 + rb, :]
                          + wa[1:2] * s_scr[cb, halo + r0:halo + r0 + rb, :]
                          + wa[2:3] * s_scr[cb, halo + 1 + r0:halo + 1 + r0 + rb, :])
                ya = p_scr[c, jj, halo + r0:halo + r0 + rb, :] * conv_a
                z_scr[r0:r0 + rb, lanes] = ya.astype(BF16)
                acc = bias + wb[0:1] * ub_scr[cb, 1 + r0:1 + r0 + rb, :]
                for k in range(1, B_CONV):
                    acc = acc + wb[k:k + 1] * ub_scr[cb, 1 + k + r0:1 + k + r0 + rb, :]
                yb_scr[r0:r0 + rb, lanes] = acc

    for r in range(T // rb):
        sl = slice(r * rb, (r + 1) * rb)
        v = yb_scr[sl, :]
        mu = jnp.mean(v, axis=-1, keepdims=True)
        dv = v - mu
        var = jnp.mean(dv * dv, axis=-1, keepdims=True)
        y = dv * lax.rsqrt(var + EPS) * lng_ref[0] + lnb_ref[0]
        z_scr[sl, a_width:] = _silu(y).astype(BF16)

    y = jnp.dot(z_scr[...], wout_ref[0], preferred_element_type=F32)
    rbo = 64
    for r in range(T // rbo):
        sl = slice(r * rbo, (r + 1) * rbo)
        o_ref[0, sl, :] = x_ref[0, sl, :] + gate * y[sl, :]


def _mixer(x, mod, layer, e, norm_w, win, ca, cb, cbb, lng, lnb, wout, T, CH):
    bsz, seq, d = x.shape
    width = ca.shape[2]
    n_chunks = width // CH
    nT = seq // T
    halo = 16
    in_specs = _halo_specs(T, halo, d, nT) + [
        _mod_spec(mod, d),
        _layer_spec(norm_w, layer),
        _layer_spec(win, e),
        _layer_spec(ca, e),
        _layer_spec(cb, e),
        _layer_spec(cbb, e),
        _layer_spec(lng, e),
        _layer_spec(lnb, e),
        _layer_spec(wout, e),
    ]
    ncb_total = width // LANES
    return pl.pallas_call(
        functools.partial(_mixer_kernel, T=T, CH=CH),
        grid=(bsz, nT),
        in_specs=in_specs,
        out_specs=pl.BlockSpec((1, T, d), lambda b, i: (b, i, 0)),
        out_shape=jax.ShapeDtypeStruct((bsz, seq, d), F32),
        scratch_shapes=[
            pltpu.VMEM((T + 2 * halo, d), BF16),
            pltpu.VMEM((n_chunks, 5 * CH // LANES, T + 2 * halo, LANES), F32),
            pltpu.VMEM((ncb_total, T + 2 * halo, LANES), F32),
            pltpu.VMEM((ncb_total, T + 2 * halo, LANES), F32),
            pltpu.VMEM((T, width), F32),
            pltpu.VMEM((T, 2 * width), BF16),
        ],
        compiler_params=pltpu.CompilerParams(
            dimension_semantics=("arbitrary", "arbitrary"), vmem_limit_bytes=VMEM_LIMIT),
        name="conv_mixer",
    )(x, x, x, mod, norm_w, win, ca, cb, cbb, lng, lnb, wout)


def _qkv_kernel(x_ref, mod_ref, nw_ref, wqk_ref, wvt_ref, *rest, T, q_w, rope):
    if rope:
        cos_ref, sin_ref, q_ref, kd_ref, vt_ref, h_scr = rest
    else:
        q_ref, kd_ref, vt_ref, h_scr = rest
    shift = mod_ref[0, 0:1, :]
    scale = mod_ref[0, 1:2, :]
    gain = nw_ref[0]
    rbn = 32
    for r in range(T // rbn):
        sl = slice(r * rbn, (r + 1) * rbn)
        h_scr[sl, :] = _norm_mod(x_ref[0, sl, :], gain, shift, scale).astype(BF16)
    h = h_scr[...]
    qk = jnp.dot(h, wqk_ref[...], preferred_element_type=F32)
    vt = lax.dot_general(wvt_ref[...], h, NT_DIMS, preferred_element_type=F32)
    ones = jnp.ones((V_ROWS - HEAD_DIM, BLOCK), BF16)
    for jb in range(T // BLOCK):
        for hh in range(vt.shape[0] // HEAD_DIM):
            vt_ref[0, jb, hh * V_ROWS:hh * V_ROWS + HEAD_DIM, :] = (
                vt[hh * HEAD_DIM:(hh + 1) * HEAD_DIM, jb * BLOCK:(jb + 1) * BLOCK].astype(BF16))
            vt_ref[0, jb, hh * V_ROWS + HEAD_DIM:(hh + 1) * V_ROWS, :] = ones

    rb = 64
    lane = lax.broadcasted_iota(jnp.int32, (rb, LANES), 1)
    first_half = (lane % HEAD_DIM) < (HEAD_DIM // 2)
    q_scale = HEAD_DIM ** -0.5 * LOG2E
    for j in range(qk.shape[1] // LANES):
        for r in range(T // rb):
            sl = slice(r * rb, (r + 1) * rb)
            blk = qk[sl, j * LANES:(j + 1) * LANES]
            if rope:
                partner = jnp.where(first_half,
                                    pltpu.roll(blk, LANES - HEAD_DIM // 2, 1),
                                    pltpu.roll(blk, HEAD_DIM // 2, 1))
                blk = blk * cos_ref[sl, :] + partner * sin_ref[sl, :]
            if j * LANES < q_w:
                q_ref[0, sl, j * LANES:(j + 1) * LANES] = (blk * q_scale).astype(BF16)
            else:
                kd_ref[0, sl, j * LANES - q_w:(j + 1) * LANES - q_w] = blk.astype(BF16)


def _qkv(x, mod, layer, norm_w, wqk, wvt, tables, T):
    bsz, seq, d = x.shape
    kv_w = wvt.shape[0]
    qk_w = wqk.shape[1]
    q_w = qk_w - 2 * kv_w
    nT = seq // T
    vt_rows = kv_w // HEAD_DIM * V_ROWS
    rope = tables is not None
    in_specs = [
        pl.BlockSpec((1, T, d), lambda b, i: (b, i, 0)),
        _mod_spec(mod, d),
        _layer_spec(norm_w, layer),
        _const_spec((d, qk_w)),
        _const_spec((kv_w, d)),
    ]
    args = [x, mod, norm_w, wqk, wvt]
    if rope:
        in_specs += [pl.BlockSpec((T, LANES), lambda b, i: (i, 0))] * 2
        args += list(tables)
    return pl.pallas_call(
        functools.partial(_qkv_kernel, T=T, q_w=q_w, rope=rope),
        grid=(bsz, nT),
        in_specs=in_specs,
        out_specs=[
            pl.BlockSpec((1, T, q_w), lambda b, i: (b, i, 0)),
            pl.BlockSpec((1, T, 2 * kv_w), lambda b, i: (b, i, 0)),
            pl.BlockSpec((1, T // BLOCK, vt_rows, BLOCK), lambda b, i: (b, i, 0, 0)),
        ],
        out_shape=[
            jax.ShapeDtypeStruct((bsz, seq, q_w), BF16),
            jax.ShapeDtypeStruct((bsz, seq, 2 * kv_w), BF16),
            jax.ShapeDtypeStruct((bsz, seq // BLOCK, vt_rows, BLOCK), BF16),
        ],
        scratch_shapes=[pltpu.VMEM((T, d), BF16)],
        compiler_params=pltpu.CompilerParams(
            dimension_semantics=("arbitrary", "arbitrary"), vmem_limit_bytes=VMEM_LIMIT),
        name="qkv_rope",
    )(*args)


def _rope_tables(length):
    rows = length // GRID_W
    row = jnp.repeat(jnp.arange(rows), GRID_W).astype(F32)
    col = jnp.tile(jnp.arange(GRID_W), rows).astype(F32)
    n_freq = HEAD_DIM // 4
    inv_freq = ROPE_THETA ** (-jnp.arange(n_freq, dtype=F32) / n_freq)
    ang = jnp.concatenate([row[:, None] * inv_freq, col[:, None] * inv_freq], axis=-1)
    cos, sin = jnp.cos(ang), jnp.sin(ang)
    reps = LANES // HEAD_DIM
    return (jnp.tile(jnp.concatenate([cos, cos], axis=-1), (1, reps)),
            jnp.tile(jnp.concatenate([-sin, sin], axis=-1), (1, reps)))


def _attn_kernel(x_ref, mod_ref, q_ref, *rest, TQ, n_ctx, local):
    if local:
        (kdp_ref, kdm_ref, kdn_ref, vtp_ref, vtm_ref, vtn_ref, kdc_ref, vtc_ref, wo_ref, sink_ref,
         o_ref, klo_scr, khi_scr, ot_scr) = rest
    else:
        kdc_ref, vtc_ref, wo_ref, sink_ref, o_ref, klo_scr, khi_scr, ot_scr = rest
    i = pl.program_id(1)
    nq = TQ // BLOCK
    n_blocks = pl.num_programs(1) * nq
    gate = mod_ref[0, 2:3, :]

    def put_keys(k, r0):
        lane = lax.broadcasted_iota(jnp.int32, k.shape, 1)
        low = (lane % LANES) < HEAD_DIM
        zero = jnp.zeros_like(k)
        klo_scr[r0:r0 + k.shape[0], :] = jnp.where(low, k, zero)
        khi_scr[r0:r0 + k.shape[0], :] = jnp.where(low, zero, k)

    put_keys(kdc_ref[0], 0)
    if local:
        put_keys(kdp_ref[0], n_ctx)
        put_keys(kdm_ref[0], n_ctx + BLOCK)
        put_keys(kdn_ref[0], n_ctx + BLOCK + TQ)

    def v_blocks(refs, vrows):
        return jnp.concatenate([ref[0, j, vrows, :] for ref, j in refs], axis=1)

    n_loc = 3 * BLOCK
    edge = (BLOCK, BLOCK)
    kk = lax.broadcasted_iota(jnp.int32, edge, 0)
    qq = lax.broadcasted_iota(jnp.int32, edge, 1)

    def scores(n, h):
        hl = slice(h * LANES, (h + 1) * LANES)
        qs = slice(n * BLOCK, (n + 1) * BLOCK)
        qcol = h * Q_GROUP * HEAD_DIM
        q2 = jnp.concatenate([q_ref[0, qs, qcol:qcol + LANES],
                              q_ref[0, qs, qcol + LANES:qcol + 2 * LANES]], axis=0)
        out = []
        for ks in (klo_scr, khi_scr):
            sc = lax.dot_general(ks[0:n_ctx, hl], q2, NT_DIMS, preferred_element_type=F32)
            sl = None
            if local:
                r0 = n_ctx + n * BLOCK
                sl = lax.dot_general(ks[r0:r0 + n_loc, hl], q2, NT_DIMS, preferred_element_type=F32)
            out.append((sc, sl))
        return out

    def finish(n, h, s):
        qs = slice(n * BLOCK, (n + 1) * BLOCK)
        if local:
            gb = i * nq + n
            keep_prev = kk >= qq + jnp.where(gb > 0, 0, BLOCK)
            keep_next = kk <= qq - jnp.where(gb < n_blocks - 1, 0, BLOCK)
        p_ctx, p_loc, sink_term = [], [], []
        for g in range(Q_GROUP):
            cols = slice((g // 2) * BLOCK, (g // 2 + 1) * BLOCK)
            sink = jnp.full((1, BLOCK), sink_ref[h * Q_GROUP + g], F32) * LOG2E
            sc, sl = s[g % 2]
            sc = sc[:, cols]
            m = jnp.max(sc, axis=0, keepdims=True)
            if local:
                s_prev = jnp.where(keep_prev, sl[0:BLOCK, cols], NEG_INF)
                s_own = sl[BLOCK:2 * BLOCK, cols]
                s_next = jnp.where(keep_next, sl[2 * BLOCK:n_loc, cols], NEG_INF)
                m = jnp.maximum(m, jnp.max(jnp.maximum(jnp.maximum(s_prev, s_own), s_next),
                                           axis=0, keepdims=True))
            m = jnp.maximum(m, sink)
            p_ctx.append(jnp.exp2(sc - m).astype(BF16))
            if local:
                p_loc.append(jnp.concatenate([jnp.exp2(s_prev - m).astype(BF16),
                                              jnp.exp2(s_own - m).astype(BF16),
                                              jnp.exp2(s_next - m).astype(BF16)], axis=0))
            sink_term.append(jnp.exp2(sink - m))
        vrows = slice(h * V_ROWS, (h + 1) * V_ROWS)
        v_ctx = v_blocks([(vtc_ref, j) for j in range(n_ctx // BLOCK)], vrows)
        ot = jnp.dot(v_ctx, jnp.concatenate(p_ctx, axis=1), preferred_element_type=F32)
        if local:
            tile_blocks = [(vtp_ref, 0)] + [(vtm_ref, j) for j in range(nq)] + [(vtn_ref, 0)]
            ot = ot + jnp.dot(v_blocks(tile_blocks[n:n + 3], vrows),
                              jnp.concatenate(p_loc, axis=1), preferred_element_type=F32)
        den = ot[HEAD_DIM:HEAD_DIM + 1, :] + jnp.concatenate(sink_term, axis=1)
        ot = ot[0:HEAD_DIM, :] / den
        for g in range(Q_GROUP):
            hrow = (h * Q_GROUP + g) * HEAD_DIM
            ot_scr[hrow:hrow + HEAD_DIM, qs] = ot[:, g * BLOCK:(g + 1) * BLOCK].astype(BF16)

    pairs = [(n, h) for n in range(nq) for h in range(N_KV_HEADS)]
    pending = scores(*pairs[0])
    for k, (n, h) in enumerate(pairs):
        current = pending
        if k + 1 < len(pairs):
            pending = scores(*pairs[k + 1])
        finish(n, h, current)

    y = lax.dot_general(ot_scr[...], wo_ref[...], TN_DIMS, preferred_element_type=F32)
    rb = 64
    for r in range(TQ // rb):
        sl = slice(r * rb, (r + 1) * rb)
        o_ref[0, sl, :] = x_ref[0, sl, :] + gate * y[sl, :]


def _attention(x, mod, q, kd, vt, kdc, vtc, wo, sinks, TQ, local):
    bsz, seq, d = x.shape
    q_w = q.shape[2]
    kd_w = kdc.shape[2]
    vt_rows = vtc.shape[2]
    n_ctx = kdc.shape[1]
    nT = seq // TQ
    per = TQ // BLOCK
    nblk = seq // BLOCK
    in_specs = [
        pl.BlockSpec((1, TQ, d), lambda b, i: (b, i, 0)),
        _mod_spec(mod, d),
        pl.BlockSpec((1, TQ, q_w), lambda b, i: (b, i, 0)),
    ]
    args = [x, mod, q]
    if local:
        in_specs += [
            pl.BlockSpec((1, BLOCK, kd_w), lambda b, i: (b, jnp.maximum(i * per - 1, 0), 0)),
            pl.BlockSpec((1, TQ, kd_w), lambda b, i: (b, i, 0)),
            pl.BlockSpec((1, BLOCK, kd_w), lambda b, i: (b, jnp.minimum((i + 1) * per, nblk - 1), 0)),
            pl.BlockSpec((1, 1, vt_rows, BLOCK), lambda b, i: (b, jnp.maximum(i * per - 1, 0), 0, 0)),
            pl.BlockSpec((1, per, vt_rows, BLOCK), lambda b, i: (b, i, 0, 0)),
            pl.BlockSpec((1, 1, vt_rows, BLOCK), lambda b, i: (b, jnp.minimum((i + 1) * per, nblk - 1), 0, 0)),
        ]
        args += [kd, kd, kd, vt, vt, vt]
    in_specs += [
        pl.BlockSpec((1, n_ctx, kd_w), lambda b, i: (b, 0, 0)),
        pl.BlockSpec((1, n_ctx // BLOCK, vt_rows, BLOCK), lambda b, i: (b, 0, 0, 0)),
        _const_spec((q_w, d)),
        pl.BlockSpec(memory_space=pltpu.SMEM),
    ]
    args += [kdc, vtc, wo, sinks]
    key_rows = n_ctx + (TQ + 2 * BLOCK if local else 0)
    return pl.pallas_call(
        functools.partial(_attn_kernel, TQ=TQ, n_ctx=n_ctx, local=local),
        grid=(bsz, nT),
        in_specs=in_specs,
        out_specs=pl.BlockSpec((1, TQ, d), lambda b, i: (b, i, 0)),
        out_shape=jax.ShapeDtypeStruct((bsz, seq, d), F32),
        scratch_shapes=[
            pltpu.VMEM((key_rows, kd_w), BF16),
            pltpu.VMEM((key_rows, kd_w), BF16),
            pltpu.VMEM((q_w, TQ), BF16),
        ],
        compiler_params=pltpu.CompilerParams(
            dimension_semantics=("arbitrary", "arbitrary"), vmem_limit_bytes=VMEM_LIMIT),
        name="window_attn" if local else "ctx_attn",
    )(*args)


def kernel(x, c, ctx, c_ctx, w_mod, b_mod, norm_mix, norm_ffn, w_in_ab, conv_a, conv_b, conv_b_bias,
           ln_b_gain, ln_b_bias, w_out_ab, w_qkv, w_o, sinks, w_up, w_conv_ffn, w_down, final_norm):
    bsz, seq, d = x.shape
    n_ctx = ctx.shape[1]
    depth = w_mod.shape[0]
    q_w = w_o.shape[1]
    kv_w = (w_qkv.shape[2] - q_w) // 2

    T_FFN, T_MIX, T_QKV, TQ = 512, 1024, 1024, 512
    CW = 256
    CH = 256

    cond = jnp.zeros((16, d), F32).at[:bsz].set(c).at[bsz].set(c_ctx)
    mods = _modulation(cond, w_mod, b_mod).reshape(depth, 16, 6, d)

    tables = _rope_tables(seq)
    win_bf, wout_bf = w_in_ab.astype(BF16), w_out_ab.astype(BF16)
    wup_bf, wdn_bf = w_up.astype(BF16), w_down.astype(BF16)
    norm_mix3, norm_ffn3 = norm_mix[:, None, :], norm_ffn[:, None, :]
    margs = (norm_mix3, win_bf, conv_a, conv_b, conv_b_bias[:, None, :], ln_b_gain[:, None, :],
             ln_b_bias[:, None, :], wout_bf)
    fargs = (norm_ffn3, wup_bf, w_conv_ffn, wdn_bf)

    xc = ctx
    for l in range(depth):
        last = l == depth - 1
        mod_x = mods[l, :bsz]
        mod_c = mods[l, bsz:bsz + 1]
        if l % 2 == 0:
            e = l // 2
            x = _mixer(x, mod_x, l, e, *margs, T=T_MIX, CH=CH)
            if not last:
                xc = _mixer(xc, mod_c, l, e, *margs, T=n_ctx, CH=CH)
        else:
            o = l // 2
            wq = w_qkv[o][:, :q_w]
            wk = w_qkv[o][:, q_w:q_w + kv_w]
            wv = w_qkv[o][:, q_w + kv_w:]
            wk_dup = jnp.broadcast_to(wk.reshape(d, N_KV_HEADS, 1, HEAD_DIM),
                                      (d, N_KV_HEADS, 2, HEAD_DIM)).reshape(d, 2 * kv_w)
            wqk = jnp.concatenate([wq, wk_dup], axis=1).astype(BF16)
            wvt = wv.T.astype(BF16)
            wo = w_o[o].astype(BF16)
            q, kd, vt = _qkv(x, mod_x, l, norm_mix3, wqk, wvt, tables, T=T_QKV)
            qc, kdc, vtc = _qkv(xc, mod_c, l, norm_mix3, wqk, wvt, None, T=n_ctx)
            x = _attention(x, mod_x, q, kd, vt, kdc, vtc, wo, sinks[o], TQ=TQ, local=True)
            if not last:
                xc = _attention(xc, mod_c, qc, None, None, kdc, vtc, wo, sinks[o], TQ=n_ctx, local=False)
        x = _ffn(x, mod_x, l, *fargs, final_norm if last else None, T=T_FFN, CW=CW)
        if not last:
            xc = _ffn(xc, mod_c, l, *fargs, None, T=n_ctx, CW=CW)
    return x
```

```python
import functools

import jax
import jax.numpy as jnp
from jax import lax
from jax.experimental import pallas as pl
from jax.experimental.pallas import tpu as pltpu

F32 = jnp.float32
BF16 = jnp.bfloat16

EPS = 1e-6
NEG_INF = -1e30
GRID_W = 64
ROPE_THETA = 10000.0
HEAD_DIM = 64
N_KV_HEADS = 4
Q_GROUP = 4
BLOCK = 128
A_CONV = 3
B_CONV = 31
FFN_CONV = 3

LOG2E = 1.4426950408889634
V_ROWS = 80

LANES = 128
VMEM_LIMIT = 56 * 1024 * 1024

NT_DIMS = (((1,), (1,)), ((), ()))
TN_DIMS = (((0,), (0,)), ((), ()))


def _const_spec(shape):
    return pl.BlockSpec(shape, lambda *_: (0,) * len(shape), pipeline_mode=pl.Buffered(1))


def _layer_spec(arr, layer):
    tail = arr.shape[1:]
    return pl.BlockSpec((1,) + tail, lambda *_: (layer,) + (0,) * len(tail), pipeline_mode=pl.Buffered(1))


def _norm_mod(x, gain_scale, shift):
    ms = jnp.mean(x * x, axis=-1, keepdims=True)
    return x * lax.rsqrt(ms + EPS) * gain_scale + shift


def _silu(x):
    return x / (1.0 + jnp.exp(-x))


def _mod_kernel(c_ref, w_ref, b_ref, o_ref):
    s = _silu(c_ref[...]).astype(BF16)
    o_ref[0] = jnp.dot(s, w_ref[0].astype(BF16), preferred_element_type=F32) + b_ref[0]


def _modulation(cond, w_mod, b_mod):
    depth, d, n = w_mod.shape
    nb = n // 4
    rows = cond.shape[0]
    return pl.pallas_call(
        _mod_kernel,
        grid=(depth, n // nb),
        in_specs=[
            pl.BlockSpec((rows, d), lambda l, j: (0, 0)),
            pl.BlockSpec((1, d, nb), lambda l, j: (l, 0, j)),
            pl.BlockSpec((1, 1, nb), lambda l, j: (l, 0, j)),
        ],
        out_specs=pl.BlockSpec((1, rows, nb), lambda l, j: (l, 0, j)),
        out_shape=jax.ShapeDtypeStruct((depth, rows, n), F32),
        compiler_params=pltpu.CompilerParams(
            dimension_semantics=("arbitrary", "arbitrary"), vmem_limit_bytes=VMEM_LIMIT),
        name="adaln_mod",
    )(cond, w_mod, b_mod.reshape(depth, 1, n))


def _fill_h(h_scr, xp_ref, x_ref, xn_ref, gain, shift, scale, T, halo):
    i = pl.program_id(1)
    last = pl.num_programs(1) - 1
    rb = 32
    gs = gain * (1.0 + scale)
    for r in range(T // rb):
        h_scr[r * rb:(r + 1) * rb, :] = _norm_mod(x_ref[0, r * rb:(r + 1) * rb, :], gs, shift).astype(BF16)
    hp = jnp.where(i > 0, _norm_mod(xp_ref[0], gs, shift), 0.0)
    hn = jnp.where(i < last, _norm_mod(xn_ref[0], gs, shift), 0.0)
    if halo == 8:
        h_scr[T:T + 16, :] = jnp.concatenate([hp, hn], axis=0).astype(BF16)
    else:
        h_scr[T:T + halo, :] = hp.astype(BF16)
        h_scr[T + halo:T + 2 * halo, :] = hn.astype(BF16)


def _halo_specs(T, halo, d, n_seq_blocks):
    per = T // halo
    return [
        pl.BlockSpec((1, halo, d), lambda b, i: (b, jnp.maximum(i * per - 1, 0), 0)),
        pl.BlockSpec((1, T, d), lambda b, i: (b, i, 0)),
        pl.BlockSpec((1, halo, d), lambda b, i: (b, jnp.minimum((i + 1) * per, n_seq_blocks * per - 1), 0)),
    ]


def _mod_spec(mod, d):
    if mod.shape[0] == 1:
        return pl.BlockSpec((1, 6, d), lambda b, i: (0, 0, 0))
    return pl.BlockSpec((1, 6, d), lambda b, i: (b, 0, 0))


def _ffn_kernel(xp_ref, x_ref, xn_ref, mod_ref, nw_ref, wup_ref, cw_ref, wdn_ref, *rest,
                T, n_chunks, CW, final):
    if final:
        fw_ref, o_ref, h_scr, u0, u1, act_scr = rest
    else:
        o_ref, h_scr, u0, u1, act_scr = rest
    u_scr = (u0, u1)
    shift = mod_ref[0, 3:4, :]
    scale = mod_ref[0, 4:5, :]
    gate = mod_ref[0, 5:6, :]
    _fill_h(h_scr, xp_ref, x_ref, xn_ref, nw_ref[0], shift, scale, T, 8)

    ncb = CW // LANES
    d_ff = n_chunks * CW
    rb = 64

    def up(c, half):
        slot = c % 2
        col = half * d_ff + c * CW
        u = jnp.dot(h_scr[...], wup_ref[0, :, col:col + CW], preferred_element_type=F32)
        for j in range(ncb):
            uj = u[:, j * LANES:(j + 1) * LANES]
            u_scr[slot][half * ncb + j, 8:8 + T, :] = uj[0:T]
            u_scr[slot][half * ncb + j, 0:8, :] = uj[T:T + 8]
            u_scr[slot][half * ncb + j, 8 + T:16 + T, :] = uj[T + 8:T + 16]

    def convact(c, j):
        slot = c % 2
        col = c * CW + j * LANES
        wa = cw_ref[0, :, col:col + LANES]
        wg = cw_ref[0, :, d_ff + col:d_ff + col + LANES]
        for r in range(T // rb):
            r0 = r * rb

            def conv(jj, w):
                return (w[0:1] * u_scr[slot][jj, 7 + r0:7 + r0 + rb, :]
                        + w[1:2] * u_scr[slot][jj, 8 + r0:8 + r0 + rb, :]
                        + w[2:3] * u_scr[slot][jj, 9 + r0:9 + r0 + rb, :])

            a = conv(j, wa)
            g = conv(ncb + j, wg)
            act_scr[r0:r0 + rb, col:col + LANES] = (_silu(g) * a).astype(BF16)

    up(0, 0)
    up(0, 1)
    for c in range(n_chunks):
        for half in range(2):
            if c + 1 < n_chunks:
                up(c + 1, half)
            for j in range(half * ncb // 2, (half + 1) * ncb // 2):
                convact(c, j)

    y = jnp.dot(act_scr[...], wdn_ref[0], preferred_element_type=F32)
    for r in range(T // rb):
        out = x_ref[0, r * rb:(r + 1) * rb, :] + gate * y[r * rb:(r + 1) * rb, :]
        if final:
            ms = jnp.mean(out * out, axis=-1, keepdims=True)
            out = out * lax.rsqrt(ms + EPS) * fw_ref[...]
        o_ref[0, r * rb:(r + 1) * rb, :] = out


def _ffn(x, mod, layer, norm_w, wup, cw, wdn, final_w, T, CW):
    bsz, seq, d = x.shape
    d_ff = wdn.shape[1]
    n_chunks = d_ff // CW
    nT = seq // T
    final = final_w is not None
    in_specs = _halo_specs(T, 8, d, nT) + [
        _mod_spec(mod, d),
        _layer_spec(norm_w, layer),
        _layer_spec(wup, layer),
        _layer_spec(cw, layer),
        _layer_spec(wdn, layer),
    ]
    args = [x, x, x, mod, norm_w, wup, cw, wdn]
    if final:
        in_specs.append(_const_spec((1, d)))
        args.append(final_w.reshape(1, d))
    return pl.pallas_call(
        functools.partial(_ffn_kernel, T=T, n_chunks=n_chunks, CW=CW, final=final),
        grid=(bsz, nT),
        in_specs=in_specs,
        out_specs=pl.BlockSpec((1, T, d), lambda b, i: (b, i, 0)),
        out_shape=jax.ShapeDtypeStruct((bsz, seq, d), F32),
        scratch_shapes=[
            pltpu.VMEM((T + 16, d), BF16),
            pltpu.VMEM((2 * CW // LANES, T + 16, LANES), F32),
            pltpu.VMEM((2 * CW // LANES, T + 16, LANES), F32),
            pltpu.VMEM((T, d_ff), BF16),
        ],
        compiler_params=pltpu.CompilerParams(
            dimension_semantics=("arbitrary", "arbitrary"), vmem_limit_bytes=VMEM_LIMIT),
        name="conv_ffn",
    )(*args)


def _mixer_kernel(xp_ref, x_ref, xn_ref, mod_ref, nw_ref, win_ref, ca_ref, cb_ref, cbb_ref,
                  lng_ref, lnb_ref, wout_ref, o_ref,
                  h_scr, p_scr, s_scr, ub_scr, yb_scr, z_scr, *, T, CH):
    shift = mod_ref[0, 0:1, :]
    scale = mod_ref[0, 1:2, :]
    gate = mod_ref[0, 2:3, :]
    halo = 16
    _fill_h(h_scr, xp_ref, x_ref, xn_ref, nw_ref[0], shift, scale, T, halo)

    a_width = ca_ref.shape[2]
    n_chunks = a_width // CH
    ncb = CH // LANES
    rows = T + 2 * halo
    rb = 32
    rc = 64

    for c in range(n_chunks):
        for k in range(5):
            col = k * a_width + c * CH
            p = jnp.dot(h_scr[...], win_ref[0, :, col:col + CH], preferred_element_type=F32)
            for jj in range(ncb):
                pj = p[:, jj * LANES:(jj + 1) * LANES]
                p_scr[c, k * ncb + jj, halo:halo + T, :] = pj[0:T]
                p_scr[c, k * ncb + jj, 0:halo, :] = pj[T:T + halo]
                p_scr[c, k * ncb + jj, halo + T:rows, :] = pj[T + halo:rows]

    for c in range(n_chunks):
        for jj in range(ncb):
            cb = c * ncb + jj
            for r in range(rows // rb):
                sl = slice(r * rb, (r + 1) * rb)
                s_scr[cb, sl, :] = p_scr[c, 1 * ncb + jj, sl, :] * p_scr[c, 2 * ncb + jj, sl, :]
                ub_scr[cb, sl, :] = p_scr[c, 3 * ncb + jj, sl, :] / (1.0 + jnp.exp(-p_scr[c, 4 * ncb + jj, sl, :]))

    for c in range(n_chunks):
        for jj in range(ncb):
            cb = c * ncb + jj
            lanes = slice(cb * LANES, (cb + 1) * LANES)
            wa = ca_ref[0, :, lanes]
            wb = cb_ref[0, :, lanes]
            bias = cbb_ref[0, :, lanes]
            for r in range(T // rc):
                r0 = r * rc
                conv_a = (wa[0:1] * s_scr[cb, halo - 1 + r0:halo - 1 + r0 + rc, :]
                          + wa[1:2] * s_scr[cb, halo + r0:halo + r0 + rc, :]
                          + wa[2:3] * s_scr[cb, halo + 1 + r0:halo + 1 + r0 + rc, :])
                ya = p_scr[c, jj, halo + r0:halo + r0 + rc, :] * conv_a
                z_scr[r0:r0 + rc, lanes] = ya.astype(BF16)
                acc = bias + wb[0:1] * ub_scr[cb, 1 + r0:1 + r0 + rc, :]
                for k in range(1, B_CONV):
                    acc = acc + wb[k:k + 1] * ub_scr[cb, 1 + k + r0:1 + k + r0 + rc, :]
                yb_scr[r0:r0 + rc, lanes] = acc

    for r in range(T // rb):
        sl = slice(r * rb, (r + 1) * rb)
        v = yb_scr[sl, :]
        mu = jnp.mean(v, axis=-1, keepdims=True)
        dv = v - mu
        var = jnp.mean(dv * dv, axis=-1, keepdims=True)
        y = dv * lax.rsqrt(var + EPS) * lng_ref[0] + lnb_ref[0]
        z_scr[sl, a_width:] = _silu(y).astype(BF16)

    y = jnp.dot(z_scr[...], wout_ref[0], preferred_element_type=F32)
    rbo = 64
    for r in range(T // rbo):
        sl = slice(r * rbo, (r + 1) * rbo)
        o_ref[0, sl, :] = x_ref[0, sl, :] + gate * y[sl, :]


def _mixer(x, mod, layer, e, norm_w, win, ca, cb, cbb, lng, lnb, wout, T, CH):
    bsz, seq, d = x.shape
    width = ca.shape[2]
    n_chunks = width // CH
    nT = seq // T
    halo = 16
    in_specs = _halo_specs(T, halo, d, nT) + [
        _mod_spec(mod, d),
        _layer_spec(norm_w, layer),
        _layer_spec(win, e),
        _layer_spec(ca, e),
        _layer_spec(cb, e),
        _layer_spec(cbb, e),
        _layer_spec(lng, e),
        _layer_spec(lnb, e),
        _layer_spec(wout, e),
    ]
    ncb_total = width // LANES
    return pl.pallas_call(
        functools.partial(_mixer_kernel, T=T, CH=CH),
        grid=(bsz, nT),
        in_specs=in_specs,
        out_specs=pl.BlockSpec((1, T, d), lambda b, i: (b, i, 0)),
        out_shape=jax.ShapeDtypeStruct((bsz, seq, d), F32),
        scratch_shapes=[
            pltpu.VMEM((T + 2 * halo, d), BF16),
            pltpu.VMEM((n_chunks, 5 * CH // LANES, T + 2 * halo, LANES), F32),
            pltpu.VMEM((ncb_total, T + 2 * halo, LANES), F32),
            pltpu.VMEM((ncb_total, T + 2 * halo, LANES), F32),
            pltpu.VMEM((T, width), F32),
            pltpu.VMEM((T, 2 * width), BF16),
        ],
        compiler_params=pltpu.CompilerParams(
            dimension_semantics=("arbitrary", "arbitrary"), vmem_limit_bytes=VMEM_LIMIT),
        name="conv_mixer",
    )(x, x, x, mod, norm_w, win, ca, cb, cbb, lng, lnb, wout)


def _qkv_kernel(x_ref, mod_ref, nw_ref, wqk_ref, wvt_ref, *rest, T, q_w, rope):
    if rope:
        cos_ref, sin_ref, q_ref, kd_ref, vt_ref, h_scr = rest
    else:
        q_ref, kd_ref, vt_ref, h_scr = rest
    shift = mod_ref[0, 0:1, :]
    scale = mod_ref[0, 1:2, :]
    gs = nw_ref[0] * (1.0 + scale)
    rbn = 32
    for r in range(T // rbn):
        sl = slice(r * rbn, (r + 1) * rbn)
        h_scr[sl, :] = _norm_mod(x_ref[0, sl, :], gs, shift).astype(BF16)
    h = h_scr[...]
    qk = jnp.dot(h, wqk_ref[...], preferred_element_type=F32)
    vt = lax.dot_general(wvt_ref[...], h, NT_DIMS, preferred_element_type=F32)
    ones = jnp.ones((V_ROWS - HEAD_DIM, BLOCK), BF16)
    for jb in range(T // BLOCK):
        for hh in range(vt.shape[0] // HEAD_DIM):
            vt_ref[0, jb, hh * V_ROWS:hh * V_ROWS + HEAD_DIM, :] = (
                vt[hh * HEAD_DIM:(hh + 1) * HEAD_DIM, jb * BLOCK:(jb + 1) * BLOCK].astype(BF16))
            vt_ref[0, jb, hh * V_ROWS + HEAD_DIM:(hh + 1) * V_ROWS, :] = ones

    rb = 64
    lane = lax.broadcasted_iota(jnp.int32, (rb, LANES), 1)
    first_half = (lane % HEAD_DIM) < (HEAD_DIM // 2)
    q_scale = HEAD_DIM ** -0.5 * LOG2E
    for j in range(qk.shape[1] // LANES):
        for r in range(T // rb):
            sl = slice(r * rb, (r + 1) * rb)
            blk = qk[sl, j * LANES:(j + 1) * LANES]
            if rope:
                partner = jnp.where(first_half,
                                    pltpu.roll(blk, LANES - HEAD_DIM // 2, 1),
                                    pltpu.roll(blk, HEAD_DIM // 2, 1))
                blk = blk * cos_ref[sl, :] + partner * sin_ref[sl, :]
            if j * LANES < q_w:
                q_ref[0, sl, j * LANES:(j + 1) * LANES] = (blk * q_scale).astype(BF16)
            else:
                kd_ref[0, sl, j * LANES - q_w:(j + 1) * LANES - q_w] = blk.astype(BF16)


def _qkv(x, mod, layer, norm_w, wqk, wvt, tables, T):
    bsz, seq, d = x.shape
    kv_w = wvt.shape[0]
    qk_w = wqk.shape[1]
    q_w = qk_w - 2 * kv_w
    nT = seq // T
    vt_rows = kv_w // HEAD_DIM * V_ROWS
    rope = tables is not None
    shared_mod = mod.shape[0] == 1
    in_specs = [
        pl.BlockSpec((1, T, d), lambda i, b: (b, i, 0)),
        pl.BlockSpec((1, 6, d), lambda i, b: (0 if shared_mod else b, 0, 0)),
        _layer_spec(norm_w, layer),
        _const_spec((d, qk_w)),
        _const_spec((kv_w, d)),
    ]
    args = [x, mod, norm_w, wqk, wvt]
    if rope:
        in_specs += [pl.BlockSpec((T, LANES), lambda i, b: (i, 0))] * 2
        args += list(tables)
    return pl.pallas_call(
        functools.partial(_qkv_kernel, T=T, q_w=q_w, rope=rope),
        grid=(nT, bsz),
        in_specs=in_specs,
        out_specs=[
            pl.BlockSpec((1, T, q_w), lambda i, b: (b, i, 0)),
            pl.BlockSpec((1, T, 2 * kv_w), lambda i, b: (b, i, 0)),
            pl.BlockSpec((1, T // BLOCK, vt_rows, BLOCK), lambda i, b: (b, i, 0, 0)),
        ],
        out_shape=[
            jax.ShapeDtypeStruct((bsz, seq, q_w), BF16),
            jax.ShapeDtypeStruct((bsz, seq, 2 * kv_w), BF16),
            jax.ShapeDtypeStruct((bsz, seq // BLOCK, vt_rows, BLOCK), BF16),
        ],
        scratch_shapes=[pltpu.VMEM((T, d), BF16)],
        compiler_params=pltpu.CompilerParams(
            dimension_semantics=("arbitrary", "arbitrary"), vmem_limit_bytes=VMEM_LIMIT),
        name="qkv_rope",
    )(*args)


def _rope_tables(length):
    rows = length // GRID_W
    row = jnp.repeat(jnp.arange(rows), GRID_W).astype(F32)
    col = jnp.tile(jnp.arange(GRID_W), rows).astype(F32)
    n_freq = HEAD_DIM // 4
    inv_freq = ROPE_THETA ** (-jnp.arange(n_freq, dtype=F32) / n_freq)
    ang = jnp.concatenate([row[:, None] * inv_freq, col[:, None] * inv_freq], axis=-1)
    cos, sin = jnp.cos(ang), jnp.sin(ang)
    reps = LANES // HEAD_DIM
    return (jnp.tile(jnp.concatenate([cos, cos], axis=-1), (1, reps)),
            jnp.tile(jnp.concatenate([-sin, sin], axis=-1), (1, reps)))


def _attn_kernel(x_ref, mod_ref, q_ref, *rest, TQ, n_ctx, local):
    if local:
        (kdp_ref, kdm_ref, kdn_ref, vtp_ref, vtm_ref, vtn_ref, kdc_ref, vtc_ref, wo_ref, sink_ref,
         o_ref, klo_scr, khi_scr, ot_scr) = rest
    else:
        kdc_ref, vtc_ref, wo_ref, sink_ref, o_ref, klo_scr, khi_scr, ot_scr = rest
    i = pl.program_id(1)
    nq = TQ // BLOCK
    n_blocks = pl.num_programs(1) * nq
    gate = mod_ref[0, 2:3, :]

    def put_keys(k, r0):
        lane = lax.broadcasted_iota(jnp.int32, k.shape, 1)
        low = (lane % LANES) < HEAD_DIM
        zero = jnp.zeros_like(k)
        klo_scr[r0:r0 + k.shape[0], :] = jnp.where(low, k, zero)
        khi_scr[r0:r0 + k.shape[0], :] = jnp.where(low, zero, k)

    put_keys(kdc_ref[0], 0)
    if local:
        put_keys(kdp_ref[0], n_ctx)
        put_keys(kdm_ref[0], n_ctx + BLOCK)
        put_keys(kdn_ref[0], n_ctx + BLOCK + TQ)

    def v_blocks(refs, vrows):
        return jnp.concatenate([ref[0, j, vrows, :] for ref, j in refs], axis=1)

    n_loc = 3 * BLOCK
    edge = (BLOCK, BLOCK)
    kk = lax.broadcasted_iota(jnp.int32, edge, 0)
    qq = lax.broadcasted_iota(jnp.int32, edge, 1)

    def scores(n, h):
        hl = slice(h * LANES, (h + 1) * LANES)
        qs = slice(n * BLOCK, (n + 1) * BLOCK)
        qcol = h * Q_GROUP * HEAD_DIM
        q2 = jnp.concatenate([q_ref[0, qs, qcol:qcol + LANES],
                              q_ref[0, qs, qcol + LANES:qcol + 2 * LANES]], axis=0)
        out = []
        for ks in (klo_scr, khi_scr):
            sc = lax.dot_general(ks[0:n_ctx, hl], q2, NT_DIMS, preferred_element_type=F32)
            sl = None
            if local:
                r0 = n_ctx + n * BLOCK
                sl = lax.dot_general(ks[r0:r0 + n_loc, hl], q2, NT_DIMS, preferred_element_type=F32)
            out.append((sc, sl))
        return out

    def finish(n, h, s):
        qs = slice(n * BLOCK, (n + 1) * BLOCK)
        if local:
            gb = i * nq + n
            keep_prev = kk >= qq + jnp.where(gb > 0, 0, BLOCK)
            keep_next = kk <= qq - jnp.where(gb < n_blocks - 1, 0, BLOCK)
        p_ctx, p_loc, sink_term = [], [], []
        for g in range(Q_GROUP):
            cols = slice((g // 2) * BLOCK, (g // 2 + 1) * BLOCK)
            sink = jnp.full((1, BLOCK), sink_ref[h * Q_GROUP + g], F32) * LOG2E
            sc, sl = s[g % 2]
            sc = sc[:, cols]
            m = jnp.max(sc, axis=0, keepdims=True)
            if local:
                s_prev = jnp.where(keep_prev, sl[0:BLOCK, cols], NEG_INF)
                s_own = sl[BLOCK:2 * BLOCK, cols]
                s_next = jnp.where(keep_next, sl[2 * BLOCK:n_loc, cols], NEG_INF)
                m = jnp.maximum(m, jnp.max(jnp.maximum(jnp.maximum(s_prev, s_own), s_next),
                                           axis=0, keepdims=True))
            m = jnp.maximum(m, sink)
            p_ctx.append(jnp.exp2(sc - m).astype(BF16))
            if local:
                p_loc.append(jnp.concatenate([jnp.exp2(s_prev - m).astype(BF16),
                                              jnp.exp2(s_own - m).astype(BF16),
                                              jnp.exp2(s_next - m).astype(BF16)], axis=0))
            sink_term.append(jnp.exp2(sink - m))
        vrows = slice(h * V_ROWS, (h + 1) * V_ROWS)
        v_ctx = v_blocks([(vtc_ref, j) for j in range(n_ctx // BLOCK)], vrows)
        ot = jnp.dot(v_ctx, jnp.concatenate(p_ctx, axis=1), preferred_element_type=F32)
        if local:
            tile_blocks = [(vtp_ref, 0)] + [(vtm_ref, j) for j in range(nq)] + [(vtn_ref, 0)]
            ot = ot + jnp.dot(v_blocks(tile_blocks[n:n + 3], vrows),
                              jnp.concatenate(p_loc, axis=1), preferred_element_type=F32)
        den = ot[HEAD_DIM:HEAD_DIM + 1, :] + jnp.concatenate(sink_term, axis=1)
        ot = ot[0:HEAD_DIM, :] / den
        for g in range(Q_GROUP):
            hrow = (h * Q_GROUP + g) * HEAD_DIM
            ot_scr[hrow:hrow + HEAD_DIM, qs] = ot[:, g * BLOCK:(g + 1) * BLOCK].astype(BF16)

    pairs = [(n, h) for n in range(nq) for h in range(N_KV_HEADS)]
    pending = scores(*pairs[0])
    for k, (n, h) in enumerate(pairs):
        current = pending
        if k + 1 < len(pairs):
            pending = scores(*pairs[k + 1])
        finish(n, h, current)

    y = lax.dot_general(ot_scr[...], wo_ref[...], TN_DIMS, preferred_element_type=F32)
    rb = 64
    for r in range(TQ // rb):
        sl = slice(r * rb, (r + 1) * rb)
        o_ref[0, sl, :] = x_ref[0, sl, :] + gate * y[sl, :]


def _attention(x, mod, q, kd, vt, kdc, vtc, wo, sinks, TQ, local):
    bsz, seq, d = x.shape
    q_w = q.shape[2]
    kd_w = kdc.shape[2]
    vt_rows = vtc.shape[2]
    n_ctx = kdc.shape[1]
    nT = seq // TQ
    per = TQ // BLOCK
    nblk = seq // BLOCK
    in_specs = [
        pl.BlockSpec((1, TQ, d), lambda b, i: (b, i, 0)),
        _mod_spec(mod, d),
        pl.BlockSpec((1, TQ, q_w), lambda b, i: (b, i, 0)),
    ]
    args = [x, mod, q]
    if local:
        in_specs += [
            pl.BlockSpec((1, BLOCK, kd_w), lambda b, i: (b, jnp.maximum(i * per - 1, 0), 0)),
            pl.BlockSpec((1, TQ, kd_w), lambda b, i: (b, i, 0)),
            pl.BlockSpec((1, BLOCK, kd_w), lambda b, i: (b, jnp.minimum((i + 1) * per, nblk - 1), 0)),
            pl.BlockSpec((1, 1, vt_rows, BLOCK), lambda b, i: (b, jnp.maximum(i * per - 1, 0), 0, 0)),
            pl.BlockSpec((1, per, vt_rows, BLOCK), lambda b, i: (b, i, 0, 0)),
            pl.BlockSpec((1, 1, vt_rows, BLOCK), lambda b, i: (b, jnp.minimum((i + 1) * per, nblk - 1), 0, 0)),
        ]
        args += [kd, kd, kd, vt, vt, vt]
    in_specs += [
        pl.BlockSpec((1, n_ctx, kd_w), lambda b, i: (b, 0, 0)),
        pl.BlockSpec((1, n_ctx // BLOCK, vt_rows, BLOCK), lambda b, i: (b, 0, 0, 0)),
        _const_spec((q_w, d)),
        pl.BlockSpec(memory_space=pltpu.SMEM),
    ]
    args += [kdc, vtc, wo, sinks]
    key_rows = n_ctx + (TQ + 2 * BLOCK if local else 0)
    return pl.pallas_call(
        functools.partial(_attn_kernel, TQ=TQ, n_ctx=n_ctx, local=local),
        grid=(bsz, nT),
        in_specs=in_specs,
        out_specs=pl.BlockSpec((1, TQ, d), lambda b, i: (b, i, 0)),
        out_shape=jax.ShapeDtypeStruct((bsz, seq, d), F32),
        scratch_shapes=[
            pltpu.VMEM((key_rows, kd_w), BF16),
            pltpu.VMEM((key_rows, kd_w), BF16),
            pltpu.VMEM((q_w, TQ), BF16),
        ],
        compiler_params=pltpu.CompilerParams(
            dimension_semantics=("arbitrary", "arbitrary"), vmem_limit_bytes=VMEM_LIMIT),
        name="window_attn" if local else "ctx_attn",
    )(*args)


def kernel(x, c, ctx, c_ctx, w_mod, b_mod, norm_mix, norm_ffn, w_in_ab, conv_a, conv_b, conv_b_bias,
           ln_b_gain, ln_b_bias, w_out_ab, w_qkv, w_o, sinks, w_up, w_conv_ffn, w_down, final_norm):
    bsz, seq, d = x.shape
    n_ctx = ctx.shape[1]
    depth = w_mod.shape[0]
    q_w = w_o.shape[1]
    kv_w = (w_qkv.shape[2] - q_w) // 2

    T_FFN, T_MIX, T_QKV, TQ = 512, 1024, 1024, 1024
    CW = 256
    CH = 256

    cond = jnp.zeros((16, d), F32).at[:bsz].set(c).at[bsz].set(c_ctx)
    mods = _modulation(cond, w_mod, b_mod).reshape(depth, 16, 6, d)

    tables = _rope_tables(seq)
    win_bf, wout_bf = w_in_ab.astype(BF16), w_out_ab.astype(BF16)
    wup_bf, wdn_bf = w_up.astype(BF16), w_down.astype(BF16)
    norm_mix3, norm_ffn3 = norm_mix[:, None, :], norm_ffn[:, None, :]
    margs = (norm_mix3, win_bf, conv_a, conv_b, conv_b_bias[:, None, :], ln_b_gain[:, None, :],
             ln_b_bias[:, None, :], wout_bf)
    fargs = (norm_ffn3, wup_bf, w_conv_ffn, wdn_bf)

    xc = ctx
    for l in range(depth):
        last = l == depth - 1
        mod_x = mods[l, :bsz]
        mod_c = mods[l, bsz:bsz + 1]
        if l % 2 == 0:
            e = l // 2
            x = _mixer(x, mod_x, l, e, *margs, T=T_MIX, CH=CH)
            if not last:
                xc = _mixer(xc, mod_c, l, e, *margs, T=n_ctx, CH=CH)
        else:
            o = l // 2
            wq = w_qkv[o][:, :q_w]
            wk = w_qkv[o][:, q_w:q_w + kv_w]
            wv = w_qkv[o][:, q_w + kv_w:]
            wk_dup = jnp.broadcast_to(wk.reshape(d, N_KV_HEADS, 1, HEAD_DIM),
                                      (d, N_KV_HEADS, 2, HEAD_DIM)).reshape(d, 2 * kv_w)
            wqk = jnp.concatenate([wq, wk_dup], axis=1).astype(BF16)
            wvt = wv.T.astype(BF16)
            wo = w_o[o].astype(BF16)
            q, kd, vt = _qkv(x, mod_x, l, norm_mix3, wqk, wvt, tables, T=T_QKV)
            qc, kdc, vtc = _qkv(xc, mod_c, l, norm_mix3, wqk, wvt, None, T=n_ctx)
            x = _attention(x, mod_x, q, kd, vt, kdc, vtc, wo, sinks[o], TQ=TQ, local=True)
            if not last:
                xc = _attention(xc, mod_c, qc, None, None, kdc, vtc, wo, sinks[o], TQ=n_ctx, local=False)
        x = _ffn(x, mod_x, l, *fargs, final_norm if last else None, T=T_FFN, CW=CW)
        if not last:
            xc = _ffn(xc, mod_c, l, *fargs, None, T=n_ctx, CW=CW)
    return x
```
